```python
import math
import jax, jax.numpy as jnp
from jax import lax
import numpy as np

D_MODEL = 4096
BATCH = 4
SEQ = 2048
DEPTH = 4
DEC_BATCH = 16
DEC_SEQ = 16
PAST_LEN = 2048

CHUNK = 64
D_FF = 11008
EPS = 1e-6
N_BRANCH = 4
CONV_DIM = 1024
CONV_WIDTH = 3
FOX_HEADS = 8
FOX_HEAD_DIM = 128
FOX_DIM = FOX_HEADS * FOX_HEAD_DIM
Q_BLOCK = 128
FOX_F_BIAS_INIT = 3.0
GMLP_GROUPS = 8
GMLP_GROUP_DIM = 128
GMLP_DIM = GMLP_GROUPS * GMLP_GROUP_DIM
GMLP_CHUNK = 128
SSM_DIM = 1024
SSM_HEAD_DIM = 64
SSM_HEADS = SSM_DIM // SSM_HEAD_DIM
SSM_GROUPS = 2
SSM_STATE = 128
SSM_CONV = 4
SSM_CHUNK = 64
SSM_XBC = SSM_DIM + 2 * SSM_GROUPS * SSM_STATE
DT_MIN = 0.001
DT_MAX = 0.1
IN_SIZES = (CONV_DIM, CONV_DIM, CONV_DIM, FOX_DIM, FOX_DIM, FOX_DIM, FOX_HEADS,
            GMLP_DIM, GMLP_DIM, SSM_DIM, SSM_XBC, SSM_HEADS, N_BRANCH * D_MODEL)
N_IN = 3 * CONV_DIM + 3 * FOX_DIM + FOX_HEADS + 2 * GMLP_DIM + SSM_DIM + SSM_XBC + SSM_HEADS + N_BRANCH * D_MODEL

kernel_name = "hybrid_streaming_encoder_step"


def rms_norm(x, g):
    xf = x.astype(jnp.float32)
    y = xf * lax.rsqrt(jnp.mean(xf * xf, axis=-1, keepdims=True) + EPS)
    return (y * g.astype(jnp.float32)).astype(x.dtype)


def swiglu(h, w_gate, w_up, w_down):
    return (jax.nn.silu(h @ w_gate) * (h @ w_up)) @ w_down


def split_cols(z, sizes):
    out, start = [], 0
    for s in sizes:
        out.append(z[..., start:start + s])
        start += s
    return out


def causal_dwconv(buf, inp, w):
    k_w = w.shape[0]
    L = inp.shape[1]
    full = jnp.concatenate([buf.astype(inp.dtype), inp], axis=1)
    y = full[:, 0:L] * w[0]
    for i in range(1, k_w):
        y = y + full[:, i:i + L] * w[i]
    return y, full[:, full.shape[1] - (k_w - 1):]


def fox_block(q, q_pos, f_q, k, v, f_k, k_pos):
    s = jnp.einsum('bqhd,bkhd->bhqk', q, k).astype(jnp.float32) * (1.0 / math.sqrt(FOX_HEAD_DIM))
    s = s + (jnp.swapaxes(f_q, 1, 2)[:, :, :, None] - jnp.swapaxes(f_k, 1, 2)[:, :, None, :])
    s = jnp.where(k_pos[None, :] <= q_pos[:, None], s, -jnp.inf)
    p = jax.nn.softmax(s, axis=-1)
    return jnp.einsum('bhqk,bkhd->bqhd', p.astype(v.dtype), v)


def fox_prompt(q, k, v, logf):
    b, S, H, D = q.shape
    nb = S // Q_BLOCK
    F = jnp.cumsum(logf.astype(jnp.float32), axis=1)
    pos = jnp.arange(S)
    qb = jnp.swapaxes(q.reshape(b, nb, Q_BLOCK, H, D), 0, 1)
    fb = jnp.swapaxes(F.reshape(b, nb, Q_BLOCK, H), 0, 1)
    pb = pos.reshape(nb, Q_BLOCK)
    out = lax.map(lambda a: fox_block(a[0], a[2], a[1], k, v, F, pos), (qb, fb, pb))
    return jnp.swapaxes(out, 0, 1).reshape(b, S, H, D)


def fox_sample(q, k, v, logf, cache_k, cache_v, cache_logf):
    P = cache_k.shape[1]
    L = q.shape[1]
    k_all = jnp.concatenate([cache_k.astype(k.dtype), k], axis=1)
    v_all = jnp.concatenate([cache_v.astype(v.dtype), v], axis=1)
    F = jnp.cumsum(jnp.concatenate([cache_logf.astype(jnp.float32), logf.astype(jnp.float32)], axis=1), axis=1)
    return fox_block(q, P + jnp.arange(L), F[:, P:], k_all, v_all, F, jnp.arange(P + L))


def gmlp_spatial(vn, ws, bs):
    b, L, _ = vn.shape
    lc = min(L, GMLP_CHUNK)
    nc = L // lc
    idx = jnp.arange(lc)
    mask = (idx[None, :] // CHUNK) <= (idx[:, None] // CHUNK)
    w = jnp.where(mask[None], ws[:, :lc, :lc], 0.0).astype(vn.dtype)
    vr = vn.reshape(b, nc, lc, GMLP_GROUPS, GMLP_GROUP_DIM)
    s = jnp.einsum('gij,bcjgd->bcigd', w, vr) + bs[:, :lc].T[None, None, :, :, None]
    return s.reshape(b, L, GMLP_DIM)


def ssd_scan(x, dt, a, b_mat, c_mat, h0, chunk):
    bsz, L, H, P = x.shape
    nc = L // chunk

    def blk(t):
        return t.reshape((bsz, nc, chunk) + t.shape[2:])

    xc, dtc, bc, cc = blk(x), blk(dt), blk(b_mat), blk(c_mat)
    acum = jnp.cumsum(dtc * a, axis=2)
    seg = acum[:, :, :, None, :] - acum[:, :, None, :, :]
    tri = jnp.tril(jnp.ones((chunk, chunk), dtype=bool))[None, None, :, :, None]
    decay = jnp.exp(jnp.where(tri, seg, -jnp.inf))
    xdt = xc * dtc[..., None]
    scores = jnp.einsum('bcihn,bcjhn->bcijh', cc, bc) * decay
    y_diag = jnp.einsum('bcijh,bcjhp->bcihp', scores, xdt)
    to_end = jnp.exp(acum[:, :, -1:, :] - acum)
    s_chunk = jnp.einsum('bcjhn,bcjhp->bchpn', bc * to_end[..., None], xdt)
    chunk_decay = jnp.exp(acum[:, :, -1, :])

    def step(h, inp):
        s_c, d_c = inp
        return h * d_c[:, :, None, None] + s_c, h

    h_last, h_in = lax.scan(step, h0, (jnp.swapaxes(s_chunk, 0, 1), jnp.swapaxes(chunk_decay, 0, 1)))
    h_in = jnp.swapaxes(h_in, 0, 1)
    y_off = jnp.einsum('bcihn,bchpn->bcihp', cc, h_in) * jnp.exp(acum)[..., None]
    return (y_diag + y_off).reshape(bsz, L, H, P), h_last


def token_mix(h, p, conv_a_buf, ssm_conv_buf, ssm_h, fox_cache):
    f32 = jnp.float32
    bsz, L, _ = h.shape
    (xa, gb, gc, q, k, v, f_raw, u, vg, z, xbc, dt_raw, g_raw) = split_cols(h @ p['w_in'], IN_SIZES)
    conv_y, conv_a_new = causal_dwconv(conv_a_buf, gc * xa, p['conv_a_w'])
    y_a = gb * conv_y
    q = rms_norm(q.reshape(bsz, L, FOX_HEADS, FOX_HEAD_DIM), p['fox_q_norm'])
    k = rms_norm(k.reshape(bsz, L, FOX_HEADS, FOX_HEAD_DIM), p['fox_k_norm'])
    v = v.reshape(bsz, L, FOX_HEADS, FOX_HEAD_DIM)
    logf = jax.nn.log_sigmoid(f_raw.astype(f32) + p['fox_f_bias'].astype(f32))
    if fox_cache is None:
        o = fox_prompt(q, k, v, logf)
    else:
        o = fox_sample(q, k, v, logf, fox_cache[0], fox_cache[1], fox_cache[2])
    y_b = o.reshape(bsz, L, FOX_DIM)
    vn = rms_norm(jax.nn.gelu(vg), p['gmlp_v_norm'])
    y_c = jax.nn.gelu(u) * gmlp_spatial(vn, p['gmlp_ws'], p['gmlp_bs'])
    xbc_c, ssm_conv_new = causal_dwconv(ssm_conv_buf, xbc, p['ssm_conv_w'])
    xbc_c = jax.nn.silu(xbc_c + p['ssm_conv_b'])
    xs, b_m, c_m = split_cols(xbc_c, (SSM_DIM, SSM_GROUPS * SSM_STATE, SSM_GROUPS * SSM_STATE))
    rep = SSM_HEADS // SSM_GROUPS
    xs = xs.reshape(bsz, L, SSM_HEADS, SSM_HEAD_DIM).astype(f32)
    b_m = jnp.repeat(b_m.reshape(bsz, L, SSM_GROUPS, SSM_STATE), rep, axis=2).astype(f32)
    c_m = jnp.repeat(c_m.reshape(bsz, L, SSM_GROUPS, SSM_STATE), rep, axis=2).astype(f32)
    dt = jax.nn.softplus(dt_raw.astype(f32) + p['ssm_dt_bias'].astype(f32))
    a = -jnp.exp(p['ssm_a_log'].astype(f32))
    chunk = SSM_CHUNK if L % SSM_CHUNK == 0 else L
    y_ssm, ssm_h_new = ssd_scan(xs, dt, a, b_m, c_m, ssm_h.astype(f32), chunk)
    y_ssm = y_ssm + p['ssm_d'].astype(f32)[:, None] * xs
    y_ssm = (y_ssm.reshape(bsz, L, SSM_DIM) * jax.nn.silu(z.astype(f32))).astype(h.dtype)
    y_d = rms_norm(y_ssm.reshape(bsz, L, SSM_GROUPS, SSM_DIM // SSM_GROUPS),
                   p['ssm_norm'].reshape(SSM_GROUPS, SSM_DIM // SSM_GROUPS)).reshape(bsz, L, SSM_DIM)
    gates = jax.nn.sigmoid(g_raw.reshape(bsz, L, N_BRANCH, D_MODEL) + p['gate_bias'])
    merged = (gates[:, :, 0] * (y_a @ p['w_up_a']) + gates[:, :, 1] * (y_b @ p['w_up_b'])
              + gates[:, :, 2] * (y_c @ p['w_up_c']) + gates[:, :, 3] * (y_d @ p['w_up_d']))
    out = merged @ p['w_out']
    return out, (conv_a_new, ssm_conv_new, ssm_h_new.astype(h.dtype), k, v, logf, vn)


def layer(x, p, conv_a_buf, ssm_conv_buf, ssm_h, fox_cache):
    x = x + 0.5 * swiglu(rms_norm(x, p['ffn1_norm']), p['ffn1_w_gate'], p['ffn1_w_up'], p['ffn1_w_down'])
    mix, new_state = token_mix(rms_norm(x, p['mix_norm']), p, conv_a_buf, ssm_conv_buf, ssm_h, fox_cache)
    x = x + mix
    x = x + 0.5 * swiglu(rms_norm(x, p['ffn2_norm']), p['ffn2_w_gate'], p['ffn2_w_up'], p['ffn2_w_down'])
    return x, new_state


def setup_inputs(seed: int = 0) -> dict:
    key = jax.random.key(seed)
    ks = iter(jax.random.split(key, 48))

    def nrm(shape, scale=1.0):
        return scale * jax.random.normal(next(ks), shape, jnp.float32)

    def gain(shape):
        return 1.0 + 0.02 * jax.random.normal(next(ks), shape, jnp.float32)

    x_prompt = nrm((BATCH, SEQ, D_MODEL))
    x_sample = nrm((DEC_BATCH, DEC_SEQ, D_MODEL))
    cache_fox_k = nrm((DEPTH, DEC_BATCH, PAST_LEN, FOX_HEADS, FOX_HEAD_DIM))
    cache_fox_v = nrm((DEPTH, DEC_BATCH, PAST_LEN, FOX_HEADS, FOX_HEAD_DIM))
    cache_fox_logf = jax.nn.log_sigmoid(FOX_F_BIAS_INIT + nrm((DEPTH, DEC_BATCH, PAST_LEN, FOX_HEADS)))
    state_conv_a = nrm((DEPTH, DEC_BATCH, CONV_WIDTH - 1, CONV_DIM))
    state_conv_ssm = nrm((DEPTH, DEC_BATCH, SSM_CONV - 1, SSM_XBC))
    state_ssm = nrm((DEPTH, DEC_BATCH, SSM_HEADS, SSM_HEAD_DIM, SSM_STATE), 0.5)
    ffn1_norm = gain((DEPTH, D_MODEL))
    ffn1_w_gate = nrm((DEPTH, D_MODEL, D_FF), D_MODEL ** -0.5)
    ffn1_w_up = nrm((DEPTH, D_MODEL, D_FF), D_MODEL ** -0.5)
    ffn1_w_down = nrm((DEPTH, D_FF, D_MODEL), D_FF ** -0.5)
    mix_norm = gain((DEPTH, D_MODEL))
    w_in = nrm((DEPTH, D_MODEL, N_IN), D_MODEL ** -0.5)
    gate_bias = nrm((DEPTH, N_BRANCH, D_MODEL), 0.02)
    conv_a_w = nrm((DEPTH, CONV_WIDTH, CONV_DIM), CONV_WIDTH ** -0.5)
    w_up_a = nrm((DEPTH, CONV_DIM, D_MODEL), CONV_DIM ** -0.5)
    fox_q_norm = gain((DEPTH, FOX_HEAD_DIM))
    fox_k_norm = gain((DEPTH, FOX_HEAD_DIM))
    fox_f_bias = FOX_F_BIAS_INIT + nrm((DEPTH, FOX_HEADS), 0.1)
    w_up_b = nrm((DEPTH, FOX_DIM, D_MODEL), FOX_DIM ** -0.5)
    gmlp_v_norm = gain((DEPTH, GMLP_DIM))
    gmlp_ws = nrm((DEPTH, GMLP_GROUPS, GMLP_CHUNK, GMLP_CHUNK), GMLP_CHUNK ** -0.5)
    gmlp_bs = gain((DEPTH, GMLP_GROUPS, GMLP_CHUNK))
    w_up_c = nrm((DEPTH, GMLP_DIM, D_MODEL), GMLP_DIM ** -0.5)
    ssm_conv_w = nrm((DEPTH, SSM_CONV, SSM_XBC), SSM_CONV ** -0.5)
    ssm_conv_b = nrm((DEPTH, SSM_XBC), 0.02)
    u_dt = jax.random.uniform(next(ks), (DEPTH, SSM_HEADS), jnp.float32)
    dt0 = jnp.exp(u_dt * (math.log(DT_MAX) - math.log(DT_MIN)) + math.log(DT_MIN))
    ssm_dt_bias = dt0 + jnp.log(-jnp.expm1(-dt0))
    ssm_a_log = jnp.log(jax.random.uniform(next(ks), (DEPTH, SSM_HEADS), jnp.float32, 1.0, 16.0))
    ssm_d = gain((DEPTH, SSM_HEADS))
    ssm_norm = gain((DEPTH, SSM_DIM))
    w_up_d = nrm((DEPTH, SSM_DIM, D_MODEL), SSM_DIM ** -0.5)
    w_out = nrm((DEPTH, D_MODEL, D_MODEL), D_MODEL ** -0.5)
    ffn2_norm = gain((DEPTH, D_MODEL))
    ffn2_w_gate = nrm((DEPTH, D_MODEL, D_FF), D_MODEL ** -0.5)
    ffn2_w_up = nrm((DEPTH, D_MODEL, D_FF), D_MODEL ** -0.5)
    ffn2_w_down = nrm((DEPTH, D_FF, D_MODEL), D_FF ** -0.5)
    return {"x_prompt": x_prompt, "x_sample": x_sample,
            "cache_fox_k": cache_fox_k, "cache_fox_v": cache_fox_v, "cache_fox_logf": cache_fox_logf,
            "state_conv_a": state_conv_a, "state_conv_ssm": state_conv_ssm, "state_ssm": state_ssm,
            "ffn1_norm": ffn1_norm, "ffn1_w_gate": ffn1_w_gate, "ffn1_w_up": ffn1_w_up, "ffn1_w_down": ffn1_w_down,
            "mix_norm": mix_norm, "w_in": w_in, "gate_bias": gate_bias,
            "conv_a_w": conv_a_w, "w_up_a": w_up_a,
            "fox_q_norm": fox_q_norm, "fox_k_norm": fox_k_norm, "fox_f_bias": fox_f_bias, "w_up_b": w_up_b,
            "gmlp_v_norm": gmlp_v_norm, "gmlp_ws": gmlp_ws, "gmlp_bs": gmlp_bs, "w_up_c": w_up_c,
            "ssm_conv_w": ssm_conv_w, "ssm_conv_b": ssm_conv_b, "ssm_dt_bias": ssm_dt_bias,
            "ssm_a_log": ssm_a_log, "ssm_d": ssm_d, "ssm_norm": ssm_norm, "w_up_d": w_up_d,
            "w_out": w_out,
            "ffn2_norm": ffn2_norm, "ffn2_w_gate": ffn2_w_gate, "ffn2_w_up": ffn2_w_up, "ffn2_w_down": ffn2_w_down}


def reference(x_prompt, x_sample, cache_fox_k, cache_fox_v, cache_fox_logf, state_conv_a, state_conv_ssm,
              state_ssm, ffn1_norm, ffn1_w_gate, ffn1_w_up, ffn1_w_down, mix_norm, w_in, gate_bias,
              conv_a_w, w_up_a, fox_q_norm, fox_k_norm, fox_f_bias, w_up_b, gmlp_v_norm, gmlp_ws, gmlp_bs,
              w_up_c, ssm_conv_w, ssm_conv_b, ssm_dt_bias, ssm_a_log, ssm_d, ssm_norm, w_up_d, w_out,
              ffn2_norm, ffn2_w_gate, ffn2_w_up, ffn2_w_down):
    bp = x_prompt.shape[0]
    act = x_prompt.dtype
    y_p, y_s = x_prompt, x_sample
    st_p, st_s = [], []
    for l in range(DEPTH):
        p = {'ffn1_norm': ffn1_norm[l], 'ffn1_w_gate': ffn1_w_gate[l], 'ffn1_w_up': ffn1_w_up[l],
             'ffn1_w_down': ffn1_w_down[l], 'mix_norm': mix_norm[l], 'w_in': w_in[l], 'gate_bias': gate_bias[l],
             'conv_a_w': conv_a_w[l], 'w_up_a': w_up_a[l], 'fox_q_norm': fox_q_norm[l],
             'fox_k_norm': fox_k_norm[l], 'fox_f_bias': fox_f_bias[l], 'w_up_b': w_up_b[l],
             'gmlp_v_norm': gmlp_v_norm[l], 'gmlp_ws': gmlp_ws[l], 'gmlp_bs': gmlp_bs[l], 'w_up_c': w_up_c[l],
             'ssm_conv_w': ssm_conv_w[l], 'ssm_conv_b': ssm_conv_b[l], 'ssm_dt_bias': ssm_dt_bias[l],
             'ssm_a_log': ssm_a_log[l], 'ssm_d': ssm_d[l], 'ssm_norm': ssm_norm[l], 'w_up_d': w_up_d[l],
             'w_out': w_out[l], 'ffn2_norm': ffn2_norm[l], 'ffn2_w_gate': ffn2_w_gate[l],
             'ffn2_w_up': ffn2_w_up[l], 'ffn2_w_down': ffn2_w_down[l]}
        y_p, sp = layer(y_p, p,
                        jnp.zeros((bp, CONV_WIDTH - 1, CONV_DIM), act),
                        jnp.zeros((bp, SSM_CONV - 1, SSM_XBC), act),
                        jnp.zeros((bp, SSM_HEADS, SSM_HEAD_DIM, SSM_STATE), jnp.float32),
                        None)
        y_s, ss = layer(y_s, p, state_conv_a[l], state_conv_ssm[l], state_ssm[l],
                        (cache_fox_k[l], cache_fox_v[l], cache_fox_logf[l]))
        st_p.append(sp)
        st_s.append(ss)
    new_conv_a_prompt = jnp.stack([s[0] for s in st_p])
    new_conv_ssm_prompt = jnp.stack([s[1] for s in st_p])
    new_ssm_prompt = jnp.stack([s[2] for s in st_p])
    new_fox_k_prompt = jnp.stack([s[3] for s in st_p])
    new_fox_v_prompt = jnp.stack([s[4] for s in st_p])
    new_fox_logf_prompt = jnp.stack([s[5] for s in st_p])
    new_conv_a_sample = jnp.stack([s[0] for s in st_s])
    new_conv_ssm_sample = jnp.stack([s[1] for s in st_s])
    new_ssm_sample = jnp.stack([s[2] for s in st_s])
    new_fox_k_sample = jnp.stack([s[3] for s in st_s])
    new_fox_v_sample = jnp.stack([s[4] for s in st_s])
    new_fox_logf_sample = jnp.stack([s[5] for s in st_s])
    new_gmlp_v_sample = jnp.stack([s[6] for s in st_s])
    return (y_p, y_s,
            new_fox_k_prompt, new_fox_v_prompt, new_fox_logf_prompt,
            new_conv_a_prompt, new_conv_ssm_prompt, new_ssm_prompt,
            new_fox_k_sample, new_fox_v_sample, new_fox_logf_sample,
            new_conv_a_sample, new_conv_ssm_sample, new_ssm_sample,
            new_gmlp_v_sample)
```

```python
import functools
import math

import jax
import jax.numpy as jnp
from jax import lax
from jax.experimental import pallas as pl
from jax.experimental.pallas import tpu as pltpu

F32 = jnp.float32
BF16 = jnp.bfloat16

D_MODEL = 4096
BATCH = 4
SEQ = 2048
DEPTH = 4
DEC_BATCH = 16
DEC_SEQ = 16
PAST_LEN = 2048
CHUNK = 64
D_FF = 11008
EPS = 1e-6
N_BRANCH = 4
CONV_DIM = 1024
CONV_WIDTH = 3
FOX_HEADS = 8
FOX_HEAD_DIM = 128
FOX_DIM = FOX_HEADS * FOX_HEAD_DIM
GMLP_GROUPS = 8
GMLP_GROUP_DIM = 128
GMLP_DIM = GMLP_GROUPS * GMLP_GROUP_DIM
GMLP_CHUNK = 128
SSM_DIM = 1024
SSM_HEAD_DIM = 64
SSM_HEADS = SSM_DIM // SSM_HEAD_DIM
SSM_GROUPS = 2
SSM_STATE = 128
SSM_CONV = 4
SSM_XBC = SSM_DIM + 2 * SSM_GROUPS * SSM_STATE

M_P = BATCH * SEQ
M_S = DEC_BATCH * DEC_SEQ
M_ALL = M_P + M_S

OFF_F = 3 * CONV_DIM + 3 * FOX_DIM
OFF_U = OFF_F + FOX_HEADS
OFF_DT = OFF_U + 2 * GMLP_DIM + SSM_DIM + SSM_XBC
OFF_G = OFF_DT + SSM_HEADS
N_ZC = 2 * GMLP_DIM + SSM_DIM + SSM_XBC

V7X_VMEM_LIMIT_BYTES = 56 * 1024 * 1024
NEG_BIG = -1e30
ATT_SCALE = 1.0 / math.sqrt(FOX_HEAD_DIM)
SSM_Q_PROMPT = 256


def _cp(*sem):
    return pltpu.CompilerParams(dimension_semantics=sem, vmem_limit_bytes=V7X_VMEM_LIMIT_BYTES)


def _silu(x):
    return x * jax.nn.sigmoid(x)


def _rmsnorm_body(x_ref, g_ref, o_ref):
    x = x_ref[...]
    ms = jnp.mean(x * x, axis=-1, keepdims=True)
    o_ref[...] = (x * lax.rsqrt(ms + EPS) * g_ref[...]).astype(o_ref.dtype)


def rmsnorm_bf16(x, g, tm=384):
    m, d = x.shape
    return pl.pallas_call(
        _rmsnorm_body,
        grid=(m // tm,),
        in_specs=[pl.BlockSpec((tm, d), lambda i: (i, 0)),
                  pl.BlockSpec((1, d), lambda i: (0, 0))],
        out_specs=pl.BlockSpec((tm, d), lambda i: (i, 0)),
        out_shape=jax.ShapeDtypeStruct((m, d), BF16),
        compiler_params=_cp("parallel"),
        name="rmsnorm",
    )(x, g.reshape(1, d))


def _mm_body(a_ref, w_ref, *refs, nk, scale, has_res):
    if has_res:
        x_ref, o_ref, *scr = refs
    else:
        x_ref = None
        o_ref, *scr = refs
    d = jnp.dot(a_ref[...], w_ref[...], preferred_element_type=F32)

    def finish(acc):
        if has_res:
            acc = x_ref[...] + scale * acc
        o_ref[...] = acc.astype(o_ref.dtype)

    if nk == 1:
        finish(d)
    else:
        acc_ref = scr[0]
        k = pl.program_id(2)

        @pl.when(k == 0)
        def _():
            acc_ref[...] = d

        @pl.when(jnp.logical_and(k > 0, k < nk - 1))
        def _():
            acc_ref[...] += d

        @pl.when(k == nk - 1)
        def _():
            finish(acc_ref[...] + d)


def matmul(a, w, *, tm, tn, nk=1, res=None, scale=1.0, out_dtype=F32, name="matmul"):
    m, kdim = a.shape
    n = w.shape[1]
    tk = kdim // nk
    in_specs = [pl.BlockSpec((tm, tk), lambda i, j, k: (i, k)),
                pl.BlockSpec((tk, tn), lambda i, j, k: (k, j))]
    args = [a, w]
    if res is not None:
        in_specs.append(pl.BlockSpec((tm, tn), lambda i, j, k: (i, j)))
        args.append(res)
    scratch = [pltpu.VMEM((tm, tn), F32)] if nk > 1 else []
    return pl.pallas_call(
        functools.partial(_mm_body, nk=nk, scale=scale, has_res=res is not None),
        grid=(m // tm, n // tn, nk),
        in_specs=in_specs,
        out_specs=pl.BlockSpec((tm, tn), lambda i, j, k: (i, j)),
        out_shape=jax.ShapeDtypeStruct((m, n), out_dtype),
        scratch_shapes=scratch,
        compiler_params=_cp("parallel", "parallel", "arbitrary"),
        name=name,
    )(*args)


def _ffn_up_body(h_ref, wg_ref, wu_ref, o_ref):
    h = h_ref[...]
    g = jnp.dot(h, wg_ref[...], preferred_element_type=F32)
    u = jnp.dot(h, wu_ref[...], preferred_element_type=F32)
    o_ref[...] = (_silu(g) * u).astype(o_ref.dtype)


def ffn_up(h, wg, wu, tm=1056, tn=256):
    m, d = h.shape
    n = wg.shape[1]
    return pl.pallas_call(
        _ffn_up_body,
        grid=(m // tm, n // tn),
        in_specs=[pl.BlockSpec((tm, d), lambda i, j: (i, 0)),
                  pl.BlockSpec((d, tn), lambda i, j: (0, j)),
                  pl.BlockSpec((d, tn), lambda i, j: (0, j))],
        out_specs=pl.BlockSpec((tm, tn), lambda i, j: (i, j)),
        out_shape=jax.ShapeDtypeStruct((m, n), BF16),
        compiler_params=_cp("parallel", "parallel"),
        name="ffn_up",
    )(h, wg, wu)


def ffn(x, norm_g, wg, wu, wd):
    h = rmsnorm_bf16(x, norm_g)
    g = ffn_up(h, wg, wu)
    return matmul(g, wd, tm=1056, tn=512, nk=2, res=x, scale=0.5, name="ffn_down")


def _merge_body(h_ref, ya_ref, yb_ref, yc_ref, yd_ref, g0_ref, g1_ref, g2_ref, g3_ref,
                wa_ref, wb_ref, wc_ref, wd_ref, bias_ref, o_ref):
    h = h_ref[...]
    acc = None
    branches = ((ya_ref, g0_ref, wa_ref), (yb_ref, g1_ref, wb_ref),
                (yc_ref, g2_ref, wc_ref), (yd_ref, g3_ref, wd_ref))
    for b, (y_ref, g_ref, w_ref) in enumerate(branches):
        gate = jax.nn.sigmoid(jnp.dot(h, g_ref[...], preferred_element_type=F32) + bias_ref[b:b + 1, :])
        t = gate * jnp.dot(y_ref[...], w_ref[...], preferred_element_type=F32)
        acc = t if acc is None else acc + t
    o_ref[...] = acc.astype(o_ref.dtype)


def merge(h, ys, w_gate, w_ups, gate_bias, tm=528, tn=256):
    m, d = h.shape
    nb = D_MODEL // tn
    y_spec = pl.BlockSpec((tm, CONV_DIM), lambda i, j: (i, 0))
    in_specs = [pl.BlockSpec((tm, d), lambda i, j: (i, 0))] + [y_spec] * 4
    for b in range(N_BRANCH):
        in_specs.append(pl.BlockSpec((d, tn), lambda i, j, b=b: (0, b * nb + j)))
    in_specs += [pl.BlockSpec((CONV_DIM, tn), lambda i, j: (0, j))] * 4
    in_specs.append(pl.BlockSpec((N_BRANCH, tn), lambda i, j: (0, j)))
    return pl.pallas_call(
        _merge_body,
        grid=(m // tm, nb),
        in_specs=in_specs,
        out_specs=pl.BlockSpec((tm, tn), lambda i, j: (i, j)),
        out_shape=jax.ShapeDtypeStruct((m, D_MODEL), BF16),
        compiler_params=_cp("parallel", "parallel"),
        name="merge",
    )(h, *ys, w_gate, w_gate, w_gate, w_gate, *w_ups, gate_bias)


def _conv_a_body(xa_ref, gb_ref, gc_ref, buf_ref, w_ref, y_ref, nb_ref, scr_ref, *, L):
    t = gc_ref[...] * xa_ref[...]
    scr_ref[pl.ds(8, L), :] = t
    scr_ref[pl.ds(6, 2), :] = buf_ref[0]
    w = w_ref[...]
    y = scr_ref[pl.ds(6, L), :] * w[0:1] + scr_ref[pl.ds(7, L), :] * w[1:2] + t * w[2:3]
    y_ref[...] = (gb_ref[...] * y).astype(y_ref.dtype)
    nb_ref[0] = scr_ref[pl.ds(6 + L, 2), :]


def conv_a(za, buf, w, *, L, nseq, row0, cb=256):
    ncb = CONV_DIM // cb
    rb = row0 // L

    def zspec(grp):
        return pl.BlockSpec((L, cb), lambda s, c: (rb + s, grp * ncb + c))

    return pl.pallas_call(
        functools.partial(_conv_a_body, L=L),
        grid=(nseq, ncb),
        in_specs=[zspec(0), zspec(1), zspec(2),
                  pl.BlockSpec((1, CONV_WIDTH - 1, cb), lambda s, c: (s, 0, c)),
                  pl.BlockSpec((CONV_WIDTH, cb), lambda s, c: (0, c))],
        out_specs=[pl.BlockSpec((L, cb), lambda s, c: (s, c)),
                   pl.BlockSpec((1, CONV_WIDTH - 1, cb), lambda s, c: (s, 0, c))],
        out_shape=[jax.ShapeDtypeStruct((nseq * L, CONV_DIM), BF16),
                   jax.ShapeDtypeStruct((nseq, CONV_WIDTH - 1, CONV_DIM), F32)],
        scratch_shapes=[pltpu.VMEM((L + 8, cb), F32)],
        compiler_params=_cp("parallel", "parallel"),
        name="conv_a",
    )(za, za, za, buf, w)


def _qknorm_body(q_ref, k_ref, v_ref, gq_ref, gk_ref, qn_ref, kn_ref, knb_ref, vb_ref):
    gq = gq_ref[...]
    gk = gk_ref[...]
    for h in range(FOX_HEADS):
        sl = slice(h * FOX_HEAD_DIM, (h + 1) * FOX_HEAD_DIM)
        q = q_ref[:, sl]
        qn = q * lax.rsqrt(jnp.mean(q * q, axis=-1, keepdims=True) + EPS) * gq
        qn_ref[:, sl] = qn.astype(qn_ref.dtype)
        k = k_ref[:, sl]
        kn = k * lax.rsqrt(jnp.mean(k * k, axis=-1, keepdims=True) + EPS) * gk
        kn_ref[:, sl] = kn
        knb_ref[:, sl] = kn.astype(knb_ref.dtype)
    vb_ref[...] = v_ref[...].astype(vb_ref.dtype)


def qk_norm(za, gq, gk, tm=384):
    m = za.shape[0]

    def zspec(grp):
        return pl.BlockSpec((tm, FOX_DIM), lambda i: (i, grp))

    ospec = pl.BlockSpec((tm, FOX_DIM), lambda i: (i, 0))
    gspec = pl.BlockSpec((1, FOX_HEAD_DIM), lambda i: (0, 0))
    return pl.pallas_call(
        _qknorm_body,
        grid=(m // tm,),
        in_specs=[zspec(3), zspec(4), zspec(5), gspec, gspec],
        out_specs=[ospec, ospec, ospec, ospec],
        out_shape=[jax.ShapeDtypeStruct((m, FOX_DIM), BF16),
                   jax.ShapeDtypeStruct((m, FOX_DIM), F32),
                   jax.ShapeDtypeStruct((m, FOX_DIM), BF16),
                   jax.ShapeDtypeStruct((m, FOX_DIM), BF16)],
        compiler_params=_cp("parallel"),
        name="qk_norm",
    )(za, za, za, gq.reshape(1, -1), gk.reshape(1, -1))


def _flash_body(q_ref, k_ref, v_ref, fq_ref, fkt_ref, o_ref, m_ref, l_ref, acc_ref, *, tq, tk, nk):
    qi = pl.program_id(1)
    ki = pl.program_id(2)

    @pl.when(ki == 0)
    def _():
        m_ref[...] = jnp.full(m_ref.shape, NEG_BIG, F32)
        l_ref[...] = jnp.zeros(l_ref.shape, F32)
        acc_ref[...] = jnp.zeros(acc_ref.shape, F32)

    @pl.when(ki <= qi)
    def _():
        row = qi * tq + lax.broadcasted_iota(jnp.int32, (tq, tk), 0)
        col = ki * tk + lax.broadcasted_iota(jnp.int32, (tq, tk), 1)
        mask = col <= row
        fq = fq_ref[...]
        fkt = fkt_ref[...]
        for h in range(FOX_HEADS):
            sl = slice(h * FOX_HEAD_DIM, (h + 1) * FOX_HEAD_DIM)
            s = lax.dot_general(q_ref[:, sl], k_ref[:, sl], (((1,), (1,)), ((), ())),
                                preferred_element_type=F32) * ATT_SCALE
            s = s + (fq[:, h:h + 1] - fkt[h:h + 1, :])
            s = jnp.where(mask, s, NEG_BIG)
            m_prev = m_ref[h]
            m_new = jnp.maximum(m_prev, jnp.max(s, axis=-1, keepdims=True))
            alpha = jnp.exp(m_prev - m_new)
            p = jnp.exp(s - m_new)
            l_ref[h] = alpha * l_ref[h] + jnp.sum(p, axis=-1, keepdims=True)
            acc_ref[:, sl] = alpha * acc_ref[:, sl] + jnp.dot(
                p.astype(BF16), v_ref[:, sl], preferred_element_type=F32)
            m_ref[h] = m_new

    @pl.when(ki == nk - 1)
    def _():
        for h in range(FOX_HEADS):
            sl = slice(h * FOX_HEAD_DIM, (h + 1) * FOX_HEAD_DIM)
            o_ref[:, sl] = (acc_ref[:, sl] / l_ref[h]).astype(o_ref.dtype)


def fox_prompt(qn, knb, vb, f_cum, f_cum_t, tq=512, tk=512):
    nq = SEQ // tq
    nk = SEQ // tk
    kv_spec = pl.BlockSpec((tk, FOX_DIM), lambda b, qi, ki: (b * nk + jnp.minimum(ki, qi), 0))
    return pl.pallas_call(
        functools.partial(_flash_body, tq=tq, tk=tk, nk=nk),
        grid=(BATCH, nq, nk),
        in_specs=[pl.BlockSpec((tq, FOX_DIM), lambda b, qi, ki: (b * nq + qi, 0)),
                  kv_spec, kv_spec,
                  pl.BlockSpec((tq, FOX_HEADS), lambda b, qi, ki: (b * nq + qi, 0)),
                  pl.BlockSpec((FOX_HEADS, tk), lambda b, qi, ki: (0, b * nk + jnp.minimum(ki, qi)))],
        out_specs=pl.BlockSpec((tq, FOX_DIM), lambda b, qi, ki: (b * nq + qi, 0)),
        out_shape=jax.ShapeDtypeStruct((M_P, FOX_DIM), BF16),
        scratch_shapes=[pltpu.VMEM((FOX_HEADS, tq, 1), F32),
                        pltpu.VMEM((FOX_HEADS, tq, 1), F32),
                        pltpu.VMEM((tq, FOX_DIM), F32)],
        compiler_params=_cp("parallel", "parallel", "arbitrary"),
        name="fox_prompt",
    )(qn, knb, vb, f_cum, f_cum_t)


def _fox_sample_body(q_ref, kc_ref, vc_ref, kn_ref, vn_ref, bc_ref, bn_ref, o_ref):
    rows = FOX_HEADS * DEC_SEQ
    q = q_ref[...]
    qrep = jnp.concatenate([q] * FOX_HEADS, axis=0)
    r_head = lax.broadcasted_iota(jnp.int32, (rows, FOX_DIM), 0) // DEC_SEQ
    c_head = lax.broadcasted_iota(jnp.int32, (rows, FOX_DIM), 1) // FOX_HEAD_DIM
    own = r_head == c_head
    qbd = jnp.where(own, qrep, jnp.zeros_like(qrep))
    nt = (((1,), (1,)), ((), ()))
    pad = jnp.zeros((rows - DEC_SEQ, FOX_DIM), BF16)
    kn = jnp.concatenate([kn_ref[...], pad], axis=0)
    vn = jnp.concatenate([vn_ref[...], pad], axis=0)
    s_c = lax.dot_general(qbd, kc_ref[0].astype(BF16), nt, preferred_element_type=F32) * ATT_SCALE + bc_ref[0]
    s_n = lax.dot_general(qbd, kn, nt, preferred_element_type=F32) * ATT_SCALE + bn_ref[0]
    m = jnp.maximum(jnp.max(s_c, axis=-1, keepdims=True), jnp.max(s_n, axis=-1, keepdims=True))
    p_c = jnp.exp(s_c - m)
    p_n = jnp.exp(s_n - m)
    inv = 1.0 / (jnp.sum(p_c, axis=-1, keepdims=True) + jnp.sum(p_n, axis=-1, keepdims=True))
    o_full = (jnp.dot((p_c * inv).astype(BF16), vc_ref[0].astype(BF16), preferred_element_type=F32)
              + jnp.dot((p_n * inv).astype(BF16), vn, preferred_element_type=F32))
    o_full = jnp.where(own, o_full, 0.0)
    o = o_full[0:DEC_SEQ]
    for h in range(1, FOX_HEADS):
        o = o + o_full[h * DEC_SEQ:(h + 1) * DEC_SEQ]
    o_ref[...] = o.astype(o_ref.dtype)


def fox_sample(qn_s, knb_s, vb_s, cache_k, cache_v, bias_c, bias_n):
    rows = FOX_HEADS * DEC_SEQ
    new_spec = pl.BlockSpec((DEC_SEQ, FOX_DIM), lambda b: (b, 0))
    cache_spec = pl.BlockSpec((1, PAST_LEN, FOX_DIM), lambda b: (b, 0, 0))
    return pl.pallas_call(
        _fox_sample_body,
        grid=(DEC_BATCH,),
        in_specs=[new_spec, cache_spec, cache_spec, new_spec, new_spec,
                  pl.BlockSpec((1, rows, PAST_LEN), lambda b: (b, 0, 0)),
                  pl.BlockSpec((1, rows, rows), lambda b: (b, 0, 0))],
        out_specs=new_spec,
        out_shape=jax.ShapeDtypeStruct((M_S, FOX_DIM), BF16),
        compiler_params=_cp("parallel"),
        name="fox_sample",
    )(qn_s, cache_k, cache_v, knb_s, vb_s, bias_c, bias_n)


def _gmlp_body(u_ref, vg_ref, gn_ref, w_ref, b_ref, y_ref, *vn_refs):
    v = jax.nn.gelu(vg_ref[...])
    vn = v * lax.rsqrt(jnp.mean(v * v, axis=-1, keepdims=True) + EPS) * gn_ref[...]
    if vn_refs:
        vn_refs[0][...] = vn
    vb = vn.astype(BF16)
    bias = b_ref[...]
    for g in range(GMLP_GROUPS):
        sl = slice(g * GMLP_GROUP_DIM, (g + 1) * GMLP_GROUP_DIM)
        s = jnp.dot(w_ref[g], vb[:, sl], preferred_element_type=F32) + bias[:, g:g + 1]
        y_ref[:, sl] = (jax.nn.gelu(u_ref[:, sl]) * s).astype(y_ref.dtype)


def gmlp(zc, gn, w_eff, b_eff, *, R, nchunk, row0, emit_vn):
    rb = row0 // R
    out_specs = [pl.BlockSpec((R, GMLP_DIM), lambda c: (c, 0))]
    out_shape = [jax.ShapeDtypeStruct((nchunk * R, GMLP_DIM), BF16)]
    if emit_vn:
        out_specs.append(pl.BlockSpec((R, GMLP_DIM), lambda c: (c, 0)))
        out_shape.append(jax.ShapeDtypeStruct((nchunk * R, GMLP_DIM), F32))
    return pl.pallas_call(
        _gmlp_body,
        grid=(nchunk,),
        in_specs=[pl.BlockSpec((R, GMLP_DIM), lambda c: (rb + c, 0)),
                  pl.BlockSpec((R, GMLP_DIM), lambda c: (rb + c, 1)),
                  pl.BlockSpec((1, GMLP_DIM), lambda c: (0, 0)),
                  pl.BlockSpec((GMLP_GROUPS, R, R), lambda c: (0, 0, 0)),
                  pl.BlockSpec((R, GMLP_GROUPS), lambda c: (0, 0))],
        out_specs=out_specs,
        out_shape=out_shape,
        compiler_params=_cp("parallel"),
        name="gmlp",
    )(zc, zc, gn.reshape(1, -1), w_eff, b_eff)


def _ssm_body(z_ref, xbc_ref, dt_ref, adtt_ref, a_ref, cbuf_ref, h0_ref, cw_ref, cb_ref, dfull_ref, nw_ref,
              y_ref, ncb_ref, hs_out_ref, scr_ref, hs_ref, *, Q):
    c = pl.program_id(1)
    nc = pl.num_programs(1)
    tail = SSM_CONV - 1
    lo = 8 - tail

    @pl.when(c == 0)
    def _():
        scr_ref[pl.ds(lo, tail), :] = cbuf_ref[0]
        hs_ref[...] = h0_ref[0]

    xbc = xbc_ref[...]
    scr_ref[pl.ds(8, Q), :] = xbc
    cw = cw_ref[...]
    conv = xbc * cw[tail:tail + 1] + cb_ref[...]
    for i in range(tail):
        conv = conv + scr_ref[pl.ds(lo + i, Q), :] * cw[i:i + 1]
    xc = _silu(conv)
    new_tail = scr_ref[pl.ds(lo + Q, tail), :]
    ncb_ref[0] = new_tail
    scr_ref[pl.ds(lo, tail), :] = new_tail

    xs = xc[:, :SSM_DIM]
    gw = SSM_STATE
    dt = dt_ref[...]
    adt = dt * a_ref[...]
    ii = lax.broadcasted_iota(jnp.int32, (Q, Q), 0)
    jj = lax.broadcasted_iota(jnp.int32, (Q, Q), 1)
    tri = ii >= jj
    acum = jnp.dot(tri.astype(F32), adt, precision=lax.Precision.HIGHEST,
                   preferred_element_type=F32)
    acum_t = jnp.dot(adtt_ref[0], (ii <= jj).astype(F32), precision=lax.Precision.HIGHEST,
                     preferred_element_type=F32)
    a_last = acum[Q - 1:Q, :]
    lane_lo = lax.broadcasted_iota(jnp.int32, (Q, 2 * SSM_HEAD_DIM), 1) < SSM_HEAD_DIM
    row_lo = lax.broadcasted_iota(jnp.int32, (2 * SSM_HEAD_DIM, SSM_STATE), 0) < SSM_HEAD_DIM
    nt = (((1,), (1,)), ((), ()))
    tn = (((0,), (0,)), ((), ()))
    dfull = dfull_ref[...]
    ys = []
    s_g = None
    for pr in range(SSM_HEADS // 2):
        g = (2 * pr) // (SSM_HEADS // SSM_GROUPS)
        b_g = xc[:, SSM_DIM + g * gw:SSM_DIM + (g + 1) * gw]
        c_g = xc[:, SSM_DIM + SSM_GROUPS * gw + g * gw:SSM_DIM + SSM_GROUPS * gw + (g + 1) * gw]
        c_gb = c_g.astype(BF16)
        if (2 * pr) % (SSM_HEADS // SSM_GROUPS) == 0:
            s_g = lax.dot_general(c_gb, b_g.astype(BF16), nt, preferred_element_type=F32)
        h0i, h1i = 2 * pr, 2 * pr + 1
        sl = slice(pr * 2 * SSM_HEAD_DIM, (pr + 1) * 2 * SSM_HEAD_DIM)
        xs_p = xs[:, sl]
        a0 = acum[:, h0i:h0i + 1]
        a1 = acum[:, h1i:h1i + 1]
        l0 = jnp.exp(jnp.where(tri, a0 - acum_t[h0i:h0i + 1, :], -jnp.inf))
        l1 = jnp.exp(jnp.where(tri, a1 - acum_t[h1i:h1i + 1, :], -jnp.inf))
        xdt = (xs_p * jnp.where(lane_lo, dt[:, h0i:h0i + 1], dt[:, h1i:h1i + 1])).astype(BF16)
        y = jnp.where(lane_lo,
                      jnp.dot((s_g * l0).astype(BF16), xdt, preferred_element_type=F32),
                      jnp.dot((s_g * l1).astype(BF16), xdt, preferred_element_type=F32))
        hs_p = hs_ref[pr]
        y_off = lax.dot_general(c_gb, hs_p.astype(BF16), nt, preferred_element_type=F32)
        y = y + y_off * jnp.exp(jnp.where(lane_lo, a0, a1)) + dfull[:, sl] * xs_p
        ys.append(y)
        al0 = a_last[:, h0i:h0i + 1]
        al1 = a_last[:, h1i:h1i + 1]
        bw0 = (b_g * jnp.exp(al0 - a0)).astype(BF16)
        bw1 = (b_g * jnp.exp(al1 - a1)).astype(BF16)
        r0 = lax.dot_general(xdt, bw0, tn, preferred_element_type=F32)
        r1 = lax.dot_general(xdt, bw1, tn, preferred_element_type=F32)
        decay = jnp.where(row_lo, jnp.exp(al0), jnp.exp(al1))
        hs_ref[pr] = hs_p * decay + jnp.where(row_lo, r0, r1)

    z = z_ref[...]
    y_all = jnp.concatenate(ys, axis=1) * _silu(z)
    gsz = SSM_DIM // SSM_GROUPS
    nw = nw_ref[...]
    for g in range(SSM_GROUPS):
        sl = slice(g * gsz, (g + 1) * gsz)
        seg = y_all[:, sl]
        y_ref[:, sl] = (seg * lax.rsqrt(jnp.mean(seg * seg, axis=-1, keepdims=True) + EPS)
                        * nw[:, sl]).astype(y_ref.dtype)

    @pl.when(c == nc - 1)
    def _():
        hs_out_ref[0] = hs_ref[...]


def ssm(zc, dt, adt_t, a_row, cbuf, h0, cw, cb, dfull, nw, *, Q, nseq, L, row0):
    nchunk = L // Q
    rb = row0 // Q
    npair = SSM_HEADS // 2
    return pl.pallas_call(
        functools.partial(_ssm_body, Q=Q),
        grid=(nseq, nchunk),
        in_specs=[pl.BlockSpec((Q, SSM_DIM), lambda s, c: (rb + s * nchunk + c, 2)),
                  pl.BlockSpec((Q, SSM_XBC), lambda s, c: (rb + s * nchunk + c, 2)),
                  pl.BlockSpec((Q, SSM_HEADS), lambda s, c: (s * nchunk + c, 0)),
                  pl.BlockSpec((1, SSM_HEADS, Q), lambda s, c: (s * nchunk + c, 0, 0)),
                  pl.BlockSpec((1, SSM_HEADS), lambda s, c: (0, 0)),
                  pl.BlockSpec((1, SSM_CONV - 1, SSM_XBC), lambda s, c: (s, 0, 0)),
                  pl.BlockSpec((1, npair, 2 * SSM_HEAD_DIM, SSM_STATE), lambda s, c: (s, 0, 0, 0)),
                  pl.BlockSpec((SSM_CONV, SSM_XBC), lambda s, c: (0, 0)),
                  pl.BlockSpec((1, SSM_XBC), lambda s, c: (0, 0)),
                  pl.BlockSpec((1, SSM_DIM), lambda s, c: (0, 0)),
                  pl.BlockSpec((1, SSM_DIM), lambda s, c: (0, 0))],
        out_specs=[pl.BlockSpec((Q, SSM_DIM), lambda s, c: (s * nchunk + c, 0)),
                   pl.BlockSpec((1, SSM_CONV - 1, SSM_XBC), lambda s, c: (s, 0, 0)),
                   pl.BlockSpec((1, npair, 2 * SSM_HEAD_DIM, SSM_STATE), lambda s, c: (s, 0, 0, 0))],
        out_shape=[jax.ShapeDtypeStruct((nseq * L, SSM_DIM), BF16),
                   jax.ShapeDtypeStruct((nseq, SSM_CONV - 1, SSM_XBC), F32),
                   jax.ShapeDtypeStruct((nseq, npair, 2 * SSM_HEAD_DIM, SSM_STATE), F32)],
        scratch_shapes=[pltpu.VMEM((Q + 8, SSM_XBC), F32),
                        pltpu.VMEM((npair, 2 * SSM_HEAD_DIM, SSM_STATE), F32)],
        compiler_params=_cp("parallel", "arbitrary"),
        name="ssm",
    )(zc, zc, dt, adt_t, a_row, cbuf, h0, cw, cb, dfull, nw)


def _token_mix(x, l, p, st):
    h = rmsnorm_bf16(x, p['mix_norm'][l])
    w_in = p['w_in'][l]
    za = matmul(h, w_in[:, :OFF_F].astype(BF16), tm=1056, tn=512, name="in_proj_a")
    zc = matmul(h, w_in[:, OFF_U:OFF_DT].astype(BF16), tm=1056, tn=512, name="in_proj_c")
    w_small = jnp.concatenate([w_in[:, OFF_F:OFF_U], w_in[:, OFF_DT:OFF_G],
                               jnp.zeros((D_MODEL, 128 - FOX_HEADS - SSM_HEADS), F32)], axis=1).astype(BF16)
    zs = matmul(h, w_small, tm=1056, tn=128, name="in_proj_s")

    conv_w = p['conv_a_w'][l]
    ya_p, nca_p = conv_a(za, jnp.zeros((BATCH, CONV_WIDTH - 1, CONV_DIM), F32), conv_w,
                         L=SEQ, nseq=BATCH, row0=0)
    ya_s, nca_s = conv_a(za, st['conv_a'][l], conv_w, L=DEC_SEQ, nseq=DEC_BATCH, row0=M_P)

    qn, kn, knb, vb = qk_norm(za, p['fox_q_norm'][l], p['fox_k_norm'][l])
    logf = jax.nn.log_sigmoid(zs[:, :FOX_HEADS] + p['fox_f_bias'][l])
    f_p = jnp.cumsum(logf[:M_P].reshape(BATCH, SEQ, FOX_HEADS), axis=1).reshape(M_P, FOX_HEADS)
    yb_p = fox_prompt(qn, knb, vb, f_p, f_p.T)
    logf_s = logf[M_P:].reshape(DEC_BATCH, DEC_SEQ, FOX_HEADS)
    f_all = jnp.cumsum(jnp.concatenate([st['fox_logf'][l], logf_s], axis=1), axis=1)
    f_q = f_all[:, PAST_LEN:]
    bias = (jnp.transpose(f_q, (0, 2, 1))[:, :, :, None]
            - jnp.transpose(f_all, (0, 2, 1))[:, :, None, :])
    kpos = jnp.arange(PAST_LEN + DEC_SEQ)
    qpos = PAST_LEN + jnp.arange(DEC_SEQ)
    bias = jnp.where(kpos[None, :] <= qpos[:, None], bias, NEG_BIG)
    bias = bias.reshape(DEC_BATCH, FOX_HEADS * DEC_SEQ, PAST_LEN + DEC_SEQ)
    bias_n = jnp.pad(bias[:, :, PAST_LEN:], ((0, 0), (0, 0), (0, FOX_HEADS * DEC_SEQ - DEC_SEQ)),
                     constant_values=NEG_BIG)
    yb_s = fox_sample(qn[M_P:], knb[M_P:], vb[M_P:],
                      st['fox_k'][l].reshape(DEC_BATCH, PAST_LEN, FOX_DIM),
                      st['fox_v'][l].reshape(DEC_BATCH, PAST_LEN, FOX_DIM),
                      bias[:, :, :PAST_LEN], bias_n)

    ws = p['gmlp_ws'][l]
    bs = p['gmlp_bs'][l]
    idx = jnp.arange(GMLP_CHUNK)
    cmask = (idx[None, :] // CHUNK) <= (idx[:, None] // CHUNK)
    w_p = jnp.where(cmask[None], ws, 0.0).astype(BF16)
    yc_p = gmlp(zc, p['gmlp_v_norm'][l], w_p, bs.T, R=GMLP_CHUNK, nchunk=M_P // GMLP_CHUNK, row0=0,
                emit_vn=False)[0]
    eye = jnp.eye(DEC_BATCH, dtype=F32)
    w_s = jax.vmap(lambda w: jnp.kron(eye, w))(ws[:, :DEC_SEQ, :DEC_SEQ]).astype(BF16)
    b_s = jnp.tile(bs[:, :DEC_SEQ].T, (DEC_BATCH, 1))
    yc_s, vn_s = gmlp(zc, p['gmlp_v_norm'][l], w_s, b_s, R=M_S, nchunk=1, row0=M_P, emit_vn=True)

    dt = jax.nn.softplus(zs[:, FOX_HEADS:FOX_HEADS + SSM_HEADS] + p['ssm_dt_bias'][l])
    a = -jnp.exp(p['ssm_a_log'][l])
    adt = dt * a
    qp = SSM_Q_PROMPT
    adt_t_p = jnp.transpose(adt[:M_P].reshape(M_P // qp, qp, SSM_HEADS), (0, 2, 1))
    adt_t_s = jnp.transpose(adt[M_P:].reshape(DEC_BATCH, DEC_SEQ, SSM_HEADS), (0, 2, 1))
    dfull = jnp.repeat(p['ssm_d'][l], SSM_HEAD_DIM).reshape(1, SSM_DIM)
    common = (p['ssm_conv_w'][l], p['ssm_conv_b'][l].reshape(1, -1), dfull, p['ssm_norm'][l].reshape(1, -1))
    npair = SSM_HEADS // 2
    yd_p, ncs_p, hs_p = ssm(zc, dt[:M_P], adt_t_p, a.reshape(1, -1),
                            jnp.zeros((BATCH, SSM_CONV - 1, SSM_XBC), F32),
                            jnp.zeros((BATCH, npair, 2 * SSM_HEAD_DIM, SSM_STATE), F32),
                            *common, Q=qp, nseq=BATCH, L=SEQ, row0=0)
    yd_s, ncs_s, hs_s = ssm(zc, dt[M_P:], adt_t_s, a.reshape(1, -1), st['conv_ssm'][l],
                            st['ssm'][l].reshape(DEC_BATCH, npair, 2 * SSM_HEAD_DIM, SSM_STATE),
                            *common, Q=DEC_SEQ, nseq=DEC_BATCH, L=DEC_SEQ, row0=M_P)

    ys = [jnp.concatenate(pair, axis=0) for pair in ((ya_p, ya_s), (yb_p, yb_s), (yc_p, yc_s), (yd_p, yd_s))]
    merged = merge(h, ys, w_in[:, OFF_G:].astype(BF16),
                   [p[n][l].astype(BF16) for n in ('w_up_a', 'w_up_b', 'w_up_c', 'w_up_d')],
                   p['gate_bias'][l])
    x = matmul(merged, p['w_out'][l].astype(BF16), tm=1056, tn=512, res=x, scale=1.0, name="out_proj")

    v_f32 = za[:, 5 * 1024:6 * 1024]
    hshape = (SSM_HEADS, SSM_HEAD_DIM, SSM_STATE)
    new_p = (nca_p, ncs_p, hs_p.reshape((BATCH,) + hshape),
             kn[:M_P].reshape(BATCH, SEQ, FOX_HEADS, FOX_HEAD_DIM),
             v_f32[:M_P].reshape(BATCH, SEQ, FOX_HEADS, FOX_HEAD_DIM),
             logf[:M_P].reshape(BATCH, SEQ, FOX_HEADS))
    new_s = (nca_s, ncs_s, hs_s.reshape((DEC_BATCH,) + hshape),
             kn[M_P:].reshape(DEC_BATCH, DEC_SEQ, FOX_HEADS, FOX_HEAD_DIM),
             v_f32[M_P:].reshape(DEC_BATCH, DEC_SEQ, FOX_HEADS, FOX_HEAD_DIM),
             logf_s, vn_s.reshape(DEC_BATCH, DEC_SEQ, GMLP_DIM))
    return x, new_p, new_s


def kernel(x_prompt, x_sample, cache_fox_k, cache_fox_v, cache_fox_logf, state_conv_a, state_conv_ssm, state_ssm, ffn1_norm, ffn1_w_gate, ffn1_w_up, ffn1_w_down, mix_norm, w_in, gate_bias, conv_a_w, w_up_a, fox_q_norm, fox_k_norm, fox_f_bias, w_up_b, gmlp_v_norm, gmlp_ws, gmlp_bs, w_up_c, ssm_conv_w, ssm_conv_b, ssm_dt_bias, ssm_a_log, ssm_d, ssm_norm, w_up_d, w_out, ffn2_norm, ffn2_w_gate, ffn2_w_up, ffn2_w_down):
    p = dict(mix_norm=mix_norm, w_in=w_in, gate_bias=gate_bias, conv_a_w=conv_a_w, w_up_a=w_up_a,
             fox_q_norm=fox_q_norm, fox_k_norm=fox_k_norm, fox_f_bias=fox_f_bias, w_up_b=w_up_b,
             gmlp_v_norm=gmlp_v_norm, gmlp_ws=gmlp_ws, gmlp_bs=gmlp_bs, w_up_c=w_up_c,
             ssm_conv_w=ssm_conv_w, ssm_conv_b=ssm_conv_b, ssm_dt_bias=ssm_dt_bias, ssm_a_log=ssm_a_log,
             ssm_d=ssm_d, ssm_norm=ssm_norm, w_up_d=w_up_d, w_out=w_out)
    st = dict(fox_k=cache_fox_k, fox_v=cache_fox_v, fox_logf=cache_fox_logf,
              conv_a=state_conv_a, conv_ssm=state_conv_ssm, ssm=state_ssm)
    x = jnp.concatenate([x_prompt.reshape(M_P, D_MODEL), x_sample.reshape(M_S, D_MODEL)], axis=0)
    st_p, st_s = [], []
    for l in range(DEPTH):
        x = ffn(x, ffn1_norm[l], ffn1_w_gate[l].astype(BF16), ffn1_w_up[l].astype(BF16),
                ffn1_w_down[l].astype(BF16))
        x, new_p, new_s = _token_mix(x, l, p, st)
        x = ffn(x, ffn2_norm[l], ffn2_w_gate[l].astype(BF16), ffn2_w_up[l].astype(BF16),
                ffn2_w_down[l].astype(BF16))
        st_p.append(new_p)
        st_s.append(new_s)

    def stack(states, i):
        return jnp.stack([s[i] for s in states])

    return (x[:M_P].reshape(BATCH, SEQ, D_MODEL), x[M_P:].reshape(DEC_BATCH, DEC_SEQ, D_MODEL),
            stack(st_p, 3), stack(st_p, 4), stack(st_p, 5), stack(st_p, 0), stack(st_p, 1), stack(st_p, 2),
            stack(st_s, 3), stack(st_s, 4), stack(st_s, 5), stack(st_s, 0), stack(st_s, 1), stack(st_s, 2),
            stack(st_s, 6))
```

```python
import functools
import math

import jax
import jax.numpy as jnp
from jax import lax
from jax.experimental import pallas as pl
from jax.experimental.pallas import tpu as pltpu

F32 = jnp.float32
BF16 = jnp.bfloat16

D_MODEL = 4096
BATCH = 4
SEQ = 2048
DEPTH = 4
DEC_BATCH = 16
DEC_SEQ = 16
PAST_LEN = 2048
CHUNK = 64
D_FF = 11008
EPS = 1e-6
N_BRANCH = 4
CONV_DIM = 1024
CONV_WIDTH = 3
FOX_HEADS = 8
FOX_HEAD_DIM = 128
FOX_DIM = FOX_HEADS * FOX_HEAD_DIM
GMLP_GROUPS = 8
GMLP_GROUP_DIM = 128
GMLP_DIM = GMLP_GROUPS * GMLP_GROUP_DIM
GMLP_CHUNK = 128
SSM_DIM = 1024
SSM_HEAD_DIM = 64
SSM_HEADS = SSM_DIM // SSM_HEAD_DIM
SSM_GROUPS = 2
SSM_STATE = 128
SSM_CONV = 4
SSM_XBC = SSM_DIM + 2 * SSM_GROUPS * SSM_STATE

M_P = BATCH * SEQ
M_S = DEC_BATCH * DEC_SEQ
M_ALL = M_P + M_S

OFF_F = 3 * CONV_DIM + 3 * FOX_DIM
OFF_U = OFF_F + FOX_HEADS
OFF_DT = OFF_U + 2 * GMLP_DIM + SSM_DIM + SSM_XBC
OFF_G = OFF_DT + SSM_HEADS
N_ZC = 2 * GMLP_DIM + SSM_DIM + SSM_XBC

V7X_VMEM_LIMIT_BYTES = 56 * 1024 * 1024
NEG_BIG = -1e30
ATT_SCALE = 1.0 / math.sqrt(FOX_HEAD_DIM)
SSM_Q_PROMPT = 256


def _cp(*sem):
    return pltpu.CompilerParams(dimension_semantics=sem, vmem_limit_bytes=V7X_VMEM_LIMIT_BYTES)


def _silu(x):
    return x * jax.nn.sigmoid(x)


def _rmsnorm_body(x_ref, g_ref, o_ref):
    x = x_ref[...]
    ms = jnp.mean(x * x, axis=-1, keepdims=True)
    o_ref[...] = (x * lax.rsqrt(ms + EPS) * g_ref[...]).astype(o_ref.dtype)


def rmsnorm_bf16(x, g, tm=384):
    m, d = x.shape
    return pl.pallas_call(
        _rmsnorm_body,
        grid=(m // tm,),
        in_specs=[pl.BlockSpec((tm, d), lambda i: (i, 0)),
                  pl.BlockSpec((1, d), lambda i: (0, 0))],
        out_specs=pl.BlockSpec((tm, d), lambda i: (i, 0)),
        out_shape=jax.ShapeDtypeStruct((m, d), BF16),
        compiler_params=_cp("parallel"),
        name="rmsnorm",
    )(x, g.reshape(1, d))


def _mm_body(a_ref, w_ref, *refs, nk, scale, has_res):
    if has_res:
        x_ref, o_ref, *scr = refs
    else:
        x_ref = None
        o_ref, *scr = refs
    d = jnp.dot(a_ref[...], w_ref[...], preferred_element_type=F32)

    def finish(acc):
        if has_res:
            acc = x_ref[...] + scale * acc
        o_ref[...] = acc.astype(o_ref.dtype)

    if nk == 1:
        finish(d)
    else:
        acc_ref = scr[0]
        k = pl.program_id(2)

        @pl.when(k == 0)
        def _():
            acc_ref[...] = d

        @pl.when(jnp.logical_and(k > 0, k < nk - 1))
        def _():
            acc_ref[...] += d

        @pl.when(k == nk - 1)
        def _():
            finish(acc_ref[...] + d)


def matmul(a, w, *, tm, tn, nk=1, res=None, scale=1.0, out_dtype=F32, name="matmul"):
    m, kdim = a.shape
    n = w.shape[1]
    tk = kdim // nk
    in_specs = [pl.BlockSpec((tm, tk), lambda i, j, k: (i, k)),
                pl.BlockSpec((tk, tn), lambda i, j, k: (k, j))]
    args = [a, w]
    if res is not None:
        in_specs.append(pl.BlockSpec((tm, tn), lambda i, j, k: (i, j)))
        args.append(res)
    scratch = [pltpu.VMEM((tm, tn), F32)] if nk > 1 else []
    return pl.pallas_call(
        functools.partial(_mm_body, nk=nk, scale=scale, has_res=res is not None),
        grid=(m // tm, n // tn, nk),
        in_specs=in_specs,
        out_specs=pl.BlockSpec((tm, tn), lambda i, j, k: (i, j)),
        out_shape=jax.ShapeDtypeStruct((m, n), out_dtype),
        scratch_shapes=scratch,
        compiler_params=_cp("parallel", "parallel", "arbitrary"),
        name=name,
    )(*args)


def _mm_ws_body(a_ref, w_ref, *refs, scale, has_res):
    if has_res:
        x_ref, o_ref, wb_ref = refs
    else:
        x_ref = None
        o_ref, wb_ref = refs

    @pl.when(pl.program_id(1) == 0)
    def _():
        wb_ref[...] = w_ref[...].astype(BF16)

    d = jnp.dot(a_ref[...], wb_ref[...], preferred_element_type=F32)
    if has_res:
        d = x_ref[...] + scale * d
    o_ref[...] = d.astype(o_ref.dtype)


def matmul_ws(a, w_stack, layer, *, n, tm, tn, col0=0, res=None, scale=1.0, out_dtype=F32, name="matmul_ws"):
    m, kdim = a.shape
    cb0 = col0 // tn
    in_specs = [pl.BlockSpec((tm, kdim), lambda j, i: (i, 0)),
                pl.BlockSpec((None, kdim, tn), lambda j, i: (layer, 0, cb0 + j))]
    args = [a, w_stack]
    if res is not None:
        in_specs.append(pl.BlockSpec((tm, tn), lambda j, i: (i, j)))
        args.append(res)
    return pl.pallas_call(
        functools.partial(_mm_ws_body, scale=scale, has_res=res is not None),
        grid=(n // tn, m // tm),
        in_specs=in_specs,
        out_specs=pl.BlockSpec((tm, tn), lambda j, i: (i, j)),
        out_shape=jax.ShapeDtypeStruct((m, n), out_dtype),
        scratch_shapes=[pltpu.VMEM((kdim, tn), BF16)],
        compiler_params=_cp("arbitrary", "arbitrary"),
        name=name,
    )(*args)


def _ffn_up_body(h_ref, wg_ref, wu_ref, o_ref, wgb_ref, wub_ref):
    @pl.when(pl.program_id(1) == 0)
    def _():
        wgb_ref[...] = wg_ref[...].astype(BF16)
        wub_ref[...] = wu_ref[...].astype(BF16)

    h = h_ref[...]
    g = jnp.dot(h, wgb_ref[...], preferred_element_type=F32)
    u = jnp.dot(h, wub_ref[...], preferred_element_type=F32)
    o_ref[...] = (_silu(g) * u).astype(o_ref.dtype)


def ffn_up(h, wg_stack, wu_stack, layer, tm=1056, tn=256):
    m, d = h.shape
    n = wg_stack.shape[2]
    w_spec = pl.BlockSpec((None, d, tn), lambda j, i: (layer, 0, j))
    return pl.pallas_call(
        _ffn_up_body,
        grid=(n // tn, m // tm),
        in_specs=[pl.BlockSpec((tm, d), lambda j, i: (i, 0)), w_spec, w_spec],
        out_specs=pl.BlockSpec((tm, tn), lambda j, i: (i, j)),
        out_shape=jax.ShapeDtypeStruct((m, n), BF16),
        scratch_shapes=[pltpu.VMEM((d, tn), BF16), pltpu.VMEM((d, tn), BF16)],
        compiler_params=_cp("arbitrary", "arbitrary"),
        name="ffn_up",
    )(h, wg_stack, wu_stack)


def ffn(x, norm_g, wg_stack, wu_stack, wd, layer):
    h = rmsnorm_bf16(x, norm_g)
    g = ffn_up(h, wg_stack, wu_stack, layer)
    return matmul(g, wd, tm=1056, tn=512, nk=2, res=x, scale=0.5, name="ffn_down")


def _merge_body(h_ref, ya_ref, yb_ref, yc_ref, yd_ref, g0_ref, g1_ref, g2_ref, g3_ref,
                wa_ref, wb_ref, wc_ref, wd_ref, bias_ref, o_ref):
    h = h_ref[...]
    acc = None
    branches = ((ya_ref, g0_ref, wa_ref), (yb_ref, g1_ref, wb_ref),
                (yc_ref, g2_ref, wc_ref), (yd_ref, g3_ref, wd_ref))
    for b, (y_ref, g_ref, w_ref) in enumerate(branches):
        gate = jax.nn.sigmoid(jnp.dot(h, g_ref[...], preferred_element_type=F32) + bias_ref[b:b + 1, :])
        t = gate * jnp.dot(y_ref[...], w_ref[...], preferred_element_type=F32)
        acc = t if acc is None else acc + t
    o_ref[...] = acc.astype(o_ref.dtype)


def merge(h, ys, w_gate, w_ups, gate_bias, tm=528, tn=256):
    m, d = h.shape
    nb = D_MODEL // tn
    y_spec = pl.BlockSpec((tm, CONV_DIM), lambda i, j: (i, 0))
    in_specs = [pl.BlockSpec((tm, d), lambda i, j: (i, 0))] + [y_spec] * 4
    for b in range(N_BRANCH):
        in_specs.append(pl.BlockSpec((d, tn), lambda i, j, b=b: (0, b * nb + j)))
    in_specs += [pl.BlockSpec((CONV_DIM, tn), lambda i, j: (0, j))] * 4
    in_specs.append(pl.BlockSpec((N_BRANCH, tn), lambda i, j: (0, j)))
    return pl.pallas_call(
        _merge_body,
        grid=(m // tm, nb),
        in_specs=in_specs,
        out_specs=pl.BlockSpec((tm, tn), lambda i, j: (i, j)),
        out_shape=jax.ShapeDtypeStruct((m, D_MODEL), BF16),
        compiler_params=_cp("parallel", "parallel"),
        name="merge",
    )(h, *ys, w_gate, w_gate, w_gate, w_gate, *w_ups, gate_bias)


def _alias_rows(ybuf, in_specs, args):
    if ybuf is None:
        return {}
    in_specs.append(pl.BlockSpec(memory_space=pl.ANY))
    args.append(ybuf)
    return {len(args) - 1: 0}


def _conv_a_body(xa_ref, gb_ref, gc_ref, buf_ref, w_ref, *refs, L):
    y_ref, nb_ref, scr_ref = refs[-3:]
    t = gc_ref[...] * xa_ref[...]
    scr_ref[pl.ds(8, L), :] = t
    scr_ref[pl.ds(6, 2), :] = buf_ref[0]
    w = w_ref[...]
    y = scr_ref[pl.ds(6, L), :] * w[0:1] + scr_ref[pl.ds(7, L), :] * w[1:2] + t * w[2:3]
    y_ref[...] = (gb_ref[...] * y).astype(y_ref.dtype)
    nb_ref[0] = scr_ref[pl.ds(6 + L, 2), :]


def conv_a(za, buf, w, *, L, nseq, row0, ybuf=None, cb=256):
    ncb = CONV_DIM // cb
    rb = row0 // L

    def zspec(grp):
        return pl.BlockSpec((L, cb), lambda s, c: (rb + s, grp * ncb + c))

    in_specs = [zspec(0), zspec(1), zspec(2),
                pl.BlockSpec((1, CONV_WIDTH - 1, cb), lambda s, c: (s, 0, c)),
                pl.BlockSpec((CONV_WIDTH, cb), lambda s, c: (0, c))]
    args = [za, za, za, buf, w]
    aliases = _alias_rows(ybuf, in_specs, args)
    return pl.pallas_call(
        functools.partial(_conv_a_body, L=L),
        grid=(nseq, ncb),
        in_specs=in_specs,
        out_specs=[pl.BlockSpec((L, cb), lambda s, c: (rb + s, c)),
                   pl.BlockSpec((1, CONV_WIDTH - 1, cb), lambda s, c: (s, 0, c))],
        out_shape=[jax.ShapeDtypeStruct((M_ALL, CONV_DIM), BF16),
                   jax.ShapeDtypeStruct((nseq, CONV_WIDTH - 1, CONV_DIM), F32)],
        scratch_shapes=[pltpu.VMEM((L + 8, cb), F32)],
        input_output_aliases=aliases,
        compiler_params=_cp("parallel", "parallel"),
        name="conv_a",
    )(*args)


def _qknorm_body(q_ref, k_ref, v_ref, gq_ref, gk_ref, *refs):
    qn_ref, knb_ref, vb_ref, kn_ref, vf_ref = refs[-5:]
    gq = gq_ref[...]
    gk = gk_ref[...]
    for h in range(FOX_HEADS):
        sl = slice(h * FOX_HEAD_DIM, (h + 1) * FOX_HEAD_DIM)
        q = q_ref[:, sl]
        qn = q * lax.rsqrt(jnp.mean(q * q, axis=-1, keepdims=True) + EPS) * gq
        qn_ref[:, sl] = qn.astype(qn_ref.dtype)
        k = k_ref[:, sl]
        kn = k * lax.rsqrt(jnp.mean(k * k, axis=-1, keepdims=True) + EPS) * gk
        kn_ref[:, sl] = kn
        knb_ref[:, sl] = kn.astype(knb_ref.dtype)
    v = v_ref[...]
    vf_ref[...] = v
    vb_ref[...] = v.astype(vb_ref.dtype)


def qk_norm(za, gq, gk, *, rows, row0, tm, layer=None, kn_stack=None, v_stack=None):
    rb = row0 // tm

    def zspec(grp):
        return pl.BlockSpec((tm, FOX_DIM), lambda i: (rb + i, grp))

    ospec = pl.BlockSpec((tm, FOX_DIM), lambda i: (i, 0))
    gspec = pl.BlockSpec((1, FOX_HEAD_DIM), lambda i: (0, 0))
    in_specs = [zspec(3), zspec(4), zspec(5), gspec, gspec]
    args = [za, za, za, gq.reshape(1, -1), gk.reshape(1, -1)]
    aliases = {}
    if layer is None:
        f32_spec = ospec
        f32_shape = jax.ShapeDtypeStruct((rows, FOX_DIM), F32)
    else:
        f32_spec = pl.BlockSpec((None, tm, FOX_DIM), lambda i: (layer, i, 0))
        f32_shape = jax.ShapeDtypeStruct((DEPTH, rows, FOX_DIM), F32)
        if kn_stack is not None:
            in_specs += [pl.BlockSpec(memory_space=pl.ANY)] * 2
            args += [kn_stack, v_stack]
            aliases = {5: 3, 6: 4}
    bf_shape = jax.ShapeDtypeStruct((rows, FOX_DIM), BF16)
    return pl.pallas_call(
        _qknorm_body,
        grid=(rows // tm,),
        in_specs=in_specs,
        out_specs=[ospec, ospec, ospec, f32_spec, f32_spec],
        out_shape=[bf_shape, bf_shape, bf_shape, f32_shape, f32_shape],
        input_output_aliases=aliases,
        compiler_params=_cp("parallel"),
        name="qk_norm",
    )(*args)


def _flash_body(q_ref, k_ref, v_ref, fq_ref, fkt_ref, o_ref, m_ref, l_ref, acc_ref, *, tq, tk, nk):
    qi = pl.program_id(1)
    ki = pl.program_id(2)
    lanes = FOX_HEAD_DIM

    @pl.when(ki == 0)
    def _():
        m_ref[...] = jnp.full(m_ref.shape, NEG_BIG, F32)
        l_ref[...] = jnp.zeros(l_ref.shape, F32)
        acc_ref[...] = jnp.zeros(acc_ref.shape, F32)

    def step(on_diagonal):
        fq = fq_ref[...]
        fkt = fkt_ref[...]
        if on_diagonal:
            mask = (lax.broadcasted_iota(jnp.int32, (tq, tk), 1)
                    <= lax.broadcasted_iota(jnp.int32, (tq, tk), 0))
        for h in range(FOX_HEADS):
            sl = slice(h * FOX_HEAD_DIM, (h + 1) * FOX_HEAD_DIM)
            s = lax.dot_general(q_ref[:, sl], k_ref[:, sl], (((1,), (1,)), ((), ())),
                                preferred_element_type=F32) * ATT_SCALE
            s = s + (fq[:, h:h + 1] - fkt[h:h + 1, :])
            if on_diagonal:
                s = jnp.where(mask, s, NEG_BIG)
            m_prev = m_ref[h]
            m_new = jnp.maximum(m_prev, jnp.max(s, axis=-1, keepdims=True))
            alpha = jnp.exp(m_prev - m_new)
            p = jnp.exp(s - jnp.concatenate([m_new] * (tk // lanes), axis=1))
            l_ref[h] = alpha * l_ref[h] + jnp.sum(p, axis=-1, keepdims=True)
            acc_ref[:, sl] = alpha * acc_ref[:, sl] + jnp.dot(
                p.astype(BF16), v_ref[:, sl], preferred_element_type=F32)
            m_ref[h] = m_new

    @pl.when(ki < qi)
    def _():
        step(False)

    @pl.when(ki == qi)
    def _():
        step(True)

    @pl.when(ki == nk - 1)
    def _():
        for h in range(FOX_HEADS):
            sl = slice(h * FOX_HEAD_DIM, (h + 1) * FOX_HEAD_DIM)
            o_ref[:, sl] = (acc_ref[:, sl] / l_ref[h]).astype(o_ref.dtype)


def fox_prompt(qn, knb, vb, f_cum, f_cum_t, t=512):
    nb = SEQ // t
    kv_spec = pl.BlockSpec((t, FOX_DIM), lambda b, qi, ki: (b * nb + jnp.minimum(ki, qi), 0))
    return pl.pallas_call(
        functools.partial(_flash_body, tq=t, tk=t, nk=nb),
        grid=(BATCH, nb, nb),
        in_specs=[pl.BlockSpec((t, FOX_DIM), lambda b, qi, ki: (b * nb + qi, 0)),
                  kv_spec, kv_spec,
                  pl.BlockSpec((t, FOX_HEADS), lambda b, qi, ki: (b * nb + qi, 0)),
                  pl.BlockSpec((FOX_HEADS, t), lambda b, qi, ki: (0, b * nb + jnp.minimum(ki, qi)))],
        out_specs=pl.BlockSpec((t, FOX_DIM), lambda b, qi, ki: (b * nb + qi, 0)),
        out_shape=jax.ShapeDtypeStruct((M_ALL, FOX_DIM), BF16),
        scratch_shapes=[pltpu.VMEM((FOX_HEADS, t, FOX_HEAD_DIM), F32),
                        pltpu.VMEM((FOX_HEADS, t, FOX_HEAD_DIM), F32),
                        pltpu.VMEM((t, FOX_DIM), F32)],
        compiler_params=_cp("parallel", "parallel", "arbitrary"),
        name="fox_prompt",
    )(qn, knb, vb, f_cum, f_cum_t)


def _fox_sample_body(q_ref, kc_ref, vc_ref, kn_ref, vn_ref, bc_ref, bn_ref, *refs):
    o_ref = refs[-1]
    rows = FOX_HEADS * DEC_SEQ
    q = q_ref[...]
    qrep = jnp.concatenate([q] * FOX_HEADS, axis=0)
    r_head = lax.broadcasted_iota(jnp.int32, (rows, FOX_DIM), 0) // DEC_SEQ
    c_head = lax.broadcasted_iota(jnp.int32, (rows, FOX_DIM), 1) // FOX_HEAD_DIM
    own = r_head == c_head
    qbd = jnp.where(own, qrep, jnp.zeros_like(qrep))
    nt = (((1,), (1,)), ((), ()))
    pad = jnp.zeros((rows - DEC_SEQ, FOX_DIM), BF16)
    kn = jnp.concatenate([kn_ref[...], pad], axis=0)
    vn = jnp.concatenate([vn_ref[...], pad], axis=0)
    s_c = lax.dot_general(qbd, kc_ref[0].astype(BF16), nt, preferred_element_type=F32) * ATT_SCALE + bc_ref[0]
    s_n = lax.dot_general(qbd, kn, nt, preferred_element_type=F32) * ATT_SCALE + bn_ref[0]
    m = jnp.maximum(jnp.max(s_c, axis=-1, keepdims=True), jnp.max(s_n, axis=-1, keepdims=True))
    p_c = jnp.exp(s_c - m)
    p_n = jnp.exp(s_n - m)
    inv = 1.0 / (jnp.sum(p_c, axis=-1, keepdims=True) + jnp.sum(p_n, axis=-1, keepdims=True))
    o_full = (jnp.dot((p_c * inv).astype(BF16), vc_ref[0].astype(BF16), preferred_element_type=F32)
              + jnp.dot((p_n * inv).astype(BF16), vn, preferred_element_type=F32))
    o_full = jnp.where(own, o_full, 0.0)
    o = o_full[0:DEC_SEQ]
    for h in range(1, FOX_HEADS):
        o = o + o_full[h * DEC_SEQ:(h + 1) * DEC_SEQ]
    o_ref[...] = o.astype(o_ref.dtype)


def fox_sample(qn_s, knb_s, vb_s, cache_k, cache_v, layer, bias_c, bias_n, ybuf):
    rows = FOX_HEADS * DEC_SEQ
    new_spec = pl.BlockSpec((DEC_SEQ, FOX_DIM), lambda b: (b, 0))
    cache_spec = pl.BlockSpec((None, 1, PAST_LEN, FOX_DIM), lambda b: (layer, b, 0, 0))
    in_specs = [new_spec, cache_spec, cache_spec, new_spec, new_spec,
                pl.BlockSpec((1, rows, PAST_LEN), lambda b: (b, 0, 0)),
                pl.BlockSpec((1, rows, rows), lambda b: (b, 0, 0))]
    args = [qn_s, cache_k, cache_v, knb_s, vb_s, bias_c, bias_n]
    aliases = _alias_rows(ybuf, in_specs, args)
    return pl.pallas_call(
        _fox_sample_body,
        grid=(DEC_BATCH,),
        in_specs=in_specs,
        out_specs=pl.BlockSpec((DEC_SEQ, FOX_DIM), lambda b: (M_P // DEC_SEQ + b, 0)),
        out_shape=jax.ShapeDtypeStruct((M_ALL, FOX_DIM), BF16),
        input_output_aliases=aliases,
        compiler_params=_cp("parallel"),
        name="fox_sample",
    )(*args)


def _gmlp_body(u_ref, vg_ref, gn_ref, w_ref, b_ref, *refs, emit_vn):
    if emit_vn:
        y_ref, vn_ref = refs[-2:]
    else:
        y_ref, vn_ref = refs[-1], None
    v = jax.nn.gelu(vg_ref[...])
    vn = v * lax.rsqrt(jnp.mean(v * v, axis=-1, keepdims=True) + EPS) * gn_ref[...]
    if emit_vn:
        vn_ref[...] = vn
    vb = vn.astype(BF16)
    bias = b_ref[...]
    for g in range(GMLP_GROUPS):
        sl = slice(g * GMLP_GROUP_DIM, (g + 1) * GMLP_GROUP_DIM)
        s = jnp.dot(w_ref[g], vb[:, sl], preferred_element_type=F32) + bias[:, g:g + 1]
        y_ref[:, sl] = (jax.nn.gelu(u_ref[:, sl]) * s).astype(y_ref.dtype)


def gmlp(zc, gn, w_eff, b_eff, *, R, nchunk, row0, emit_vn, ybuf=None):
    rb = row0 // R
    out_specs = [pl.BlockSpec((R, GMLP_DIM), lambda c: (rb + c, 0))]
    out_shape = [jax.ShapeDtypeStruct((M_ALL, GMLP_DIM), BF16)]
    if emit_vn:
        out_specs.append(pl.BlockSpec((R, GMLP_DIM), lambda c: (c, 0)))
        out_shape.append(jax.ShapeDtypeStruct((nchunk * R, GMLP_DIM), F32))
    in_specs = [pl.BlockSpec((R, GMLP_DIM), lambda c: (rb + c, 0)),
                pl.BlockSpec((R, GMLP_DIM), lambda c: (rb + c, 1)),
                pl.BlockSpec((1, GMLP_DIM), lambda c: (0, 0)),
                pl.BlockSpec((GMLP_GROUPS, R, R), lambda c: (0, 0, 0)),
                pl.BlockSpec((R, GMLP_GROUPS), lambda c: (0, 0))]
    args = [zc, zc, gn.reshape(1, -1), w_eff, b_eff]
    aliases = _alias_rows(ybuf, in_specs, args)
    return pl.pallas_call(
        functools.partial(_gmlp_body, emit_vn=emit_vn),
        grid=(nchunk,),
        in_specs=in_specs,
        out_specs=out_specs,
        out_shape=out_shape,
        input_output_aliases=aliases,
        compiler_params=_cp("parallel"),
        name="gmlp",
    )(*args)


def _ssm_body(z_ref, xbc_ref, dt_ref, adtt_ref, a_ref, cbuf_ref, h0_ref, cw_ref, cb_ref, dfull_ref, nw_ref,
              *refs, Q):
    y_ref, ncb_ref, hs_out_ref, scr_ref, hs_ref = refs[-5:]
    c = pl.program_id(1)
    nc = pl.num_programs(1)
    tail = SSM_CONV - 1
    lo = 8 - tail

    @pl.when(c == 0)
    def _():
        scr_ref[pl.ds(lo, tail), :] = cbuf_ref[0]
        hs_ref[...] = h0_ref[0]

    xbc = xbc_ref[...]
    scr_ref[pl.ds(8, Q), :] = xbc
    cw = cw_ref[...]
    conv = xbc * cw[tail:tail + 1] + cb_ref[...]
    for i in range(tail):
        conv = conv + scr_ref[pl.ds(lo + i, Q), :] * cw[i:i + 1]
    xc = _silu(conv)
    new_tail = scr_ref[pl.ds(lo + Q, tail), :]
    ncb_ref[0] = new_tail
    scr_ref[pl.ds(lo, tail), :] = new_tail

    xs = xc[:, :SSM_DIM]
    gw = SSM_STATE
    dt = dt_ref[...]
    adt = dt * a_ref[...]
    ii = lax.broadcasted_iota(jnp.int32, (Q, Q), 0)
    jj = lax.broadcasted_iota(jnp.int32, (Q, Q), 1)
    tri = ii >= jj
    acum = jnp.dot(tri.astype(F32), adt, precision=lax.Precision.HIGHEST,
                   preferred_element_type=F32)
    acum_t = jnp.dot(adtt_ref[0], (ii <= jj).astype(F32), precision=lax.Precision.HIGHEST,
                     preferred_element_type=F32)
    a_last = acum[Q - 1:Q, :]
    lane_lo = lax.broadcasted_iota(jnp.int32, (Q, 2 * SSM_HEAD_DIM), 1) < SSM_HEAD_DIM
    row_lo = lax.broadcasted_iota(jnp.int32, (2 * SSM_HEAD_DIM, SSM_STATE), 0) < SSM_HEAD_DIM
    nt = (((1,), (1,)), ((), ()))
    tn = (((0,), (0,)), ((), ()))
    dfull = dfull_ref[...]
    ys = []
    s_g = None
    for pr in range(SSM_HEADS // 2):
        g = (2 * pr) // (SSM_HEADS // SSM_GROUPS)
        b_g = xc[:, SSM_DIM + g * gw:SSM_DIM + (g + 1) * gw]
        c_g = xc[:, SSM_DIM + SSM_GROUPS * gw + g * gw:SSM_DIM + SSM_GROUPS * gw + (g + 1) * gw]
        c_gb = c_g.astype(BF16)
        if (2 * pr) % (SSM_HEADS // SSM_GROUPS) == 0:
            s_g = lax.dot_general(c_gb, b_g.astype(BF16), nt, preferred_element_type=F32)
        h0i, h1i = 2 * pr, 2 * pr + 1
        sl = slice(pr * 2 * SSM_HEAD_DIM, (pr + 1) * 2 * SSM_HEAD_DIM)
        xs_p = xs[:, sl]
        a0 = acum[:, h0i:h0i + 1]
        a1 = acum[:, h1i:h1i + 1]
        l0 = jnp.exp(jnp.where(tri, a0 - acum_t[h0i:h0i + 1, :], -jnp.inf))
        l1 = jnp.exp(jnp.where(tri, a1 - acum_t[h1i:h1i + 1, :], -jnp.inf))
        xdt = (xs_p * jnp.where(lane_lo, dt[:, h0i:h0i + 1], dt[:, h1i:h1i + 1])).astype(BF16)
        y = jnp.where(lane_lo,
                      jnp.dot((s_g * l0).astype(BF16), xdt, preferred_element_type=F32),
                      jnp.dot((s_g * l1).astype(BF16), xdt, preferred_element_type=F32))
        hs_p = hs_ref[pr]
        y_off = lax.dot_general(c_gb, hs_p.astype(BF16), nt, preferred_element_type=F32)
        y = y + y_off * jnp.exp(jnp.where(lane_lo, a0, a1)) + dfull[:, sl] * xs_p
        ys.append(y)
        al0 = a_last[:, h0i:h0i + 1]
        al1 = a_last[:, h1i:h1i + 1]
        bw0 = (b_g * jnp.exp(al0 - a0)).astype(BF16)
        bw1 = (b_g * jnp.exp(al1 - a1)).astype(BF16)
        r0 = lax.dot_general(xdt, bw0, tn, preferred_element_type=F32)
        r1 = lax.dot_general(xdt, bw1, tn, preferred_element_type=F32)
        decay = jnp.where(row_lo, jnp.exp(al0), jnp.exp(al1))
        hs_ref[pr] = hs_p * decay + jnp.where(row_lo, r0, r1)

    z = z_ref[...]
    y_all = jnp.concatenate(ys, axis=1) * _silu(z)
    gsz = SSM_DIM // SSM_GROUPS
    nw = nw_ref[...]
    for g in range(SSM_GROUPS):
        sl = slice(g * gsz, (g + 1) * gsz)
        seg = y_all[:, sl]
        y_ref[:, sl] = (seg * lax.rsqrt(jnp.mean(seg * seg, axis=-1, keepdims=True) + EPS)
                        * nw[:, sl]).astype(y_ref.dtype)

    @pl.when(c == nc - 1)
    def _():
        hs_out_ref[0] = hs_ref[...]


def ssm(zc, dt, adt_t, adt_t_index, a_row, cbuf, h0, cw, cb, dfull, nw, *, Q, nseq, L, row0, ybuf=None):
    nchunk = L // Q
    rb = row0 // Q
    npair = SSM_HEADS // 2
    in_specs = [pl.BlockSpec((Q, SSM_DIM), lambda s, c: (rb + s * nchunk + c, 2)),
                pl.BlockSpec((Q, SSM_XBC), lambda s, c: (rb + s * nchunk + c, 2)),
                pl.BlockSpec((Q, SSM_HEADS), lambda s, c: (s * nchunk + c, 0)),
                pl.BlockSpec((1, SSM_HEADS, Q), adt_t_index),
                pl.BlockSpec((1, SSM_HEADS), lambda s, c: (0, 0)),
                pl.BlockSpec((1, SSM_CONV - 1, SSM_XBC), lambda s, c: (s, 0, 0)),
                pl.BlockSpec((1, npair, 2 * SSM_HEAD_DIM, SSM_STATE), lambda s, c: (s, 0, 0, 0)),
                pl.BlockSpec((SSM_CONV, SSM_XBC), lambda s, c: (0, 0)),
                pl.BlockSpec((1, SSM_XBC), lambda s, c: (0, 0)),
                pl.BlockSpec((1, SSM_DIM), lambda s, c: (0, 0)),
                pl.BlockSpec((1, SSM_DIM), lambda s, c: (0, 0))]
    args = [zc, zc, dt, adt_t, a_row, cbuf, h0, cw, cb, dfull, nw]
    aliases = _alias_rows(ybuf, in_specs, args)
    return pl.pallas_call(
        functools.partial(_ssm_body, Q=Q),
        grid=(nseq, nchunk),
        in_specs=in_specs,
        out_specs=[pl.BlockSpec((Q, SSM_DIM), lambda s, c: (rb + s * nchunk + c, 0)),
                   pl.BlockSpec((1, SSM_CONV - 1, SSM_XBC), lambda s, c: (s, 0, 0)),
                   pl.BlockSpec((1, npair, 2 * SSM_HEAD_DIM, SSM_STATE), lambda s, c: (s, 0, 0, 0))],
        out_shape=[jax.ShapeDtypeStruct((M_ALL, SSM_DIM), BF16),
                   jax.ShapeDtypeStruct((nseq, SSM_CONV - 1, SSM_XBC), F32),
                   jax.ShapeDtypeStruct((nseq, npair, 2 * SSM_HEAD_DIM, SSM_STATE), F32)],
        scratch_shapes=[pltpu.VMEM((Q + 8, SSM_XBC), F32),
                        pltpu.VMEM((npair, 2 * SSM_HEAD_DIM, SSM_STATE), F32)],
        input_output_aliases=aliases,
        compiler_params=_cp("parallel", "arbitrary"),
        name="ssm",
    )(*args)


def _token_mix(x, l, p, st, kn_stack, v_stack):
    h = rmsnorm_bf16(x, p['mix_norm'][l])
    w_in = p['w_in'][l]
    za = matmul_ws(h, p['w_in'], l, n=OFF_F, tm=1056, tn=512, name="in_proj_a")
    zc = matmul(h, w_in[:, OFF_U:OFF_DT].astype(BF16), tm=1056, tn=512, name="in_proj_c")
    w_small = jnp.concatenate([w_in[:, OFF_F:OFF_U], w_in[:, OFF_DT:OFF_G],
                               jnp.zeros((D_MODEL, 128 - FOX_HEADS - SSM_HEADS), F32)], axis=1).astype(BF16)
    zs = matmul(h, w_small, tm=1056, tn=128, name="in_proj_s")
    zs_t = zs.T
    logf_t = jax.nn.log_sigmoid(zs_t[:FOX_HEADS] + p['fox_f_bias'][l][:, None])
    dt_t = jax.nn.softplus(zs_t[FOX_HEADS:FOX_HEADS + SSM_HEADS] + p['ssm_dt_bias'][l][:, None])

    conv_w = p['conv_a_w'][l]
    ya, nca_p = conv_a(za, jnp.zeros((BATCH, CONV_WIDTH - 1, CONV_DIM), F32), conv_w,
                       L=SEQ, nseq=BATCH, row0=0)
    ya, nca_s = conv_a(za, st['conv_a'][l], conv_w, L=DEC_SEQ, nseq=DEC_BATCH, row0=M_P, ybuf=ya)

    gq, gk = p['fox_q_norm'][l], p['fox_k_norm'][l]
    qn_p, knb_p, vb_p, kn_stack, v_stack = qk_norm(za, gq, gk, rows=M_P, row0=0, tm=512, layer=l,
                                                   kn_stack=kn_stack, v_stack=v_stack)
    qn_s, knb_s, vb_s, kn_s, v_s = qk_norm(za, gq, gk, rows=M_S, row0=M_P, tm=M_S)
    f_t_p = jnp.cumsum(logf_t[:, :M_P].reshape(FOX_HEADS, BATCH, SEQ), axis=2).reshape(FOX_HEADS, M_P)
    yb = fox_prompt(qn_p, knb_p, vb_p, f_t_p.T, f_t_p)
    logf_s_t = jnp.transpose(logf_t[:, M_P:].reshape(FOX_HEADS, DEC_BATCH, DEC_SEQ), (1, 0, 2))
    cache_logf_t = jnp.transpose(st['fox_logf'][l], (0, 2, 1))
    f_all_t = jnp.cumsum(jnp.concatenate([cache_logf_t, logf_s_t], axis=2), axis=2)
    bias = f_all_t[:, :, PAST_LEN:, None] - f_all_t[:, :, None, :]
    kpos = jnp.arange(PAST_LEN + DEC_SEQ)
    qpos = PAST_LEN + jnp.arange(DEC_SEQ)
    bias = jnp.where(kpos[None, :] <= qpos[:, None], bias, NEG_BIG)
    bias = bias.reshape(DEC_BATCH, FOX_HEADS * DEC_SEQ, PAST_LEN + DEC_SEQ)
    bias_n = jnp.pad(bias[:, :, PAST_LEN:], ((0, 0), (0, 0), (0, FOX_HEADS * DEC_SEQ - DEC_SEQ)),
                     constant_values=NEG_BIG)
    yb = fox_sample(qn_s, knb_s, vb_s,
                    st['fox_k'].reshape(DEPTH, DEC_BATCH, PAST_LEN, FOX_DIM),
                    st['fox_v'].reshape(DEPTH, DEC_BATCH, PAST_LEN, FOX_DIM),
                    l, bias[:, :, :PAST_LEN], bias_n, yb)

    ws = p['gmlp_ws'][l]
    bs = p['gmlp_bs'][l]
    idx = jnp.arange(GMLP_CHUNK)
    cmask = (idx[None, :] // CHUNK) <= (idx[:, None] // CHUNK)
    w_p = jnp.where(cmask[None], ws, 0.0).astype(BF16)
    yc = gmlp(zc, p['gmlp_v_norm'][l], w_p, bs.T, R=GMLP_CHUNK, nchunk=M_P // GMLP_CHUNK, row0=0,
              emit_vn=False)[0]
    eye = jnp.eye(DEC_BATCH, dtype=F32)
    w_s = jax.vmap(lambda w: jnp.kron(eye, w))(ws[:, :DEC_SEQ, :DEC_SEQ]).astype(BF16)
    b_s = jnp.tile(bs[:, :DEC_SEQ].T, (DEC_BATCH, 1))
    yc, vn_s = gmlp(zc, p['gmlp_v_norm'][l], w_s, b_s, R=M_S, nchunk=1, row0=M_P, emit_vn=True, ybuf=yc)

    a = -jnp.exp(p['ssm_a_log'][l])
    adt_t = dt_t * a[:, None]
    dt = dt_t.T
    qp = SSM_Q_PROMPT
    nchunk_p = SEQ // qp
    adt_t_p = adt_t[None, :, :M_P]
    adt_t_s = jnp.transpose(adt_t[:, M_P:].reshape(SSM_HEADS, DEC_BATCH, DEC_SEQ), (1, 0, 2))
    dfull = jnp.repeat(p['ssm_d'][l], SSM_HEAD_DIM).reshape(1, SSM_DIM)
    common = (p['ssm_conv_w'][l], p['ssm_conv_b'][l].reshape(1, -1), dfull, p['ssm_norm'][l].reshape(1, -1))
    npair = SSM_HEADS // 2
    yd, ncs_p, hs_p = ssm(zc, dt[:M_P], adt_t_p, lambda s, c: (0, 0, s * nchunk_p + c), a.reshape(1, -1),
                          jnp.zeros((BATCH, SSM_CONV - 1, SSM_XBC), F32),
                          jnp.zeros((BATCH, npair, 2 * SSM_HEAD_DIM, SSM_STATE), F32),
                          *common, Q=qp, nseq=BATCH, L=SEQ, row0=0)
    yd, ncs_s, hs_s = ssm(zc, dt[M_P:], adt_t_s, lambda s, c: (s, 0, 0), a.reshape(1, -1), st['conv_ssm'][l],
                          st['ssm'][l].reshape(DEC_BATCH, npair, 2 * SSM_HEAD_DIM, SSM_STATE),
                          *common, Q=DEC_SEQ, nseq=DEC_BATCH, L=DEC_SEQ, row0=M_P, ybuf=yd)

    merged = merge(h, [ya, yb, yc, yd], w_in[:, OFF_G:].astype(BF16),
                   [p[n][l].astype(BF16) for n in ('w_up_a', 'w_up_b', 'w_up_c', 'w_up_d')],
                   p['gate_bias'][l])
    x = matmul_ws(merged, p['w_out'], l, n=D_MODEL, tm=1056, tn=512, res=x, scale=1.0, name="out_proj")

    logf = logf_t.T
    hshape = (SSM_HEADS, SSM_HEAD_DIM, SSM_STATE)
    new_p = (nca_p, ncs_p, hs_p.reshape((BATCH,) + hshape), logf[:M_P].reshape(BATCH, SEQ, FOX_HEADS))
    new_s = (nca_s, ncs_s, hs_s.reshape((DEC_BATCH,) + hshape),
             kn_s.reshape(DEC_BATCH, DEC_SEQ, FOX_HEADS, FOX_HEAD_DIM),
             v_s.reshape(DEC_BATCH, DEC_SEQ, FOX_HEADS, FOX_HEAD_DIM),
             logf[M_P:].reshape(DEC_BATCH, DEC_SEQ, FOX_HEADS), vn_s.reshape(DEC_BATCH, DEC_SEQ, GMLP_DIM))
    return x, new_p, new_s, kn_stack, v_stack


def kernel(x_prompt, x_sample, cache_fox_k, cache_fox_v, cache_fox_logf, state_conv_a, state_conv_ssm, state_ssm, ffn1_norm, ffn1_w_gate, ffn1_w_up, ffn1_w_down, mix_norm, w_in, gate_bias, conv_a_w, w_up_a, fox_q_norm, fox_k_norm, fox_f_bias, w_up_b, gmlp_v_norm, gmlp_ws, gmlp_bs, w_up_c, ssm_conv_w, ssm_conv_b, ssm_dt_bias, ssm_a_log, ssm_d, ssm_norm, w_up_d, w_out, ffn2_norm, ffn2_w_gate, ffn2_w_up, ffn2_w_down):
    p = dict(mix_norm=mix_norm, w_in=w_in, gate_bias=gate_bias, conv_a_w=conv_a_w, w_up_a=w_up_a,
             fox_q_norm=fox_q_norm, fox_k_norm=fox_k_norm, fox_f_bias=fox_f_bias, w_up_b=w_up_b,
             gmlp_v_norm=gmlp_v_norm, gmlp_ws=gmlp_ws, gmlp_bs=gmlp_bs, w_up_c=w_up_c,
             ssm_conv_w=ssm_conv_w, ssm_conv_b=ssm_conv_b, ssm_dt_bias=ssm_dt_bias, ssm_a_log=ssm_a_log,
             ssm_d=ssm_d, ssm_norm=ssm_norm, w_up_d=w_up_d, w_out=w_out)
    st = dict(fox_k=cache_fox_k, fox_v=cache_fox_v, fox_logf=cache_fox_logf,
              conv_a=state_conv_a, conv_ssm=state_conv_ssm, ssm=state_ssm)
    x = jnp.concatenate([x_prompt.reshape(M_P, D_MODEL), x_sample.reshape(M_S, D_MODEL)], axis=0)
    st_p, st_s = [], []
    kn_stack = v_stack = None
    for l in range(DEPTH):
        x = ffn(x, ffn1_norm[l], ffn1_w_gate, ffn1_w_up, ffn1_w_down[l].astype(BF16), l)
        x, new_p, new_s, kn_stack, v_stack = _token_mix(x, l, p, st, kn_stack, v_stack)
        x = ffn(x, ffn2_norm[l], ffn2_w_gate, ffn2_w_up, ffn2_w_down[l].astype(BF16), l)
        st_p.append(new_p)
        st_s.append(new_s)

    def stack(states, i):
        return jnp.stack([s[i] for s in states])

    kv_shape = (DEPTH, BATCH, SEQ, FOX_HEADS, FOX_HEAD_DIM)
    return (x[:M_P].reshape(BATCH, SEQ, D_MODEL), x[M_P:].reshape(DEC_BATCH, DEC_SEQ, D_MODEL),
            kn_stack.reshape(kv_shape), v_stack.reshape(kv_shape),
            stack(st_p, 3), stack(st_p, 0), stack(st_p, 1), stack(st_p, 2),
            stack(st_s, 3), stack(st_s, 4), stack(st_s, 5), stack(st_s, 0), stack(st_s, 1), stack(st_s, 2),
            stack(st_s, 6))
```

```python
import functools
import math

import jax
import jax.numpy as jnp
from jax import lax
from jax.experimental import pallas as pl
from jax.experimental.pallas import tpu as pltpu

F32 = jnp.float32
BF16 = jnp.bfloat16

D_MODEL = 4096
BATCH = 4
SEQ = 2048
DEPTH = 4
DEC_BATCH = 16
DEC_SEQ = 16
PAST_LEN = 2048
CHUNK = 64
D_FF = 11008
EPS = 1e-6
N_BRANCH = 4
CONV_DIM = 1024
CONV_WIDTH = 3
FOX_HEADS = 8
FOX_HEAD_DIM = 128
FOX_DIM = FOX_HEADS * FOX_HEAD_DIM
GMLP_GROUPS = 8
GMLP_GROUP_DIM = 128
GMLP_DIM = GMLP_GROUPS * GMLP_GROUP_DIM
GMLP_CHUNK = 128
SSM_DIM = 1024
SSM_HEAD_DIM = 64
SSM_HEADS = SSM_DIM // SSM_HEAD_DIM
SSM_GROUPS = 2
SSM_STATE = 128
SSM_CONV = 4
SSM_XBC = SSM_DIM + 2 * SSM_GROUPS * SSM_STATE

M_P = BATCH * SEQ
M_S = DEC_BATCH * DEC_SEQ
M_ALL = M_P + M_S

OFF_F = 3 * CONV_DIM + 3 * FOX_DIM
OFF_U = OFF_F + FOX_HEADS
OFF_DT = OFF_U + 2 * GMLP_DIM + SSM_DIM + SSM_XBC
OFF_G = OFF_DT + SSM_HEADS
N_ZC = 2 * GMLP_DIM + SSM_DIM + SSM_XBC

V7X_VMEM_LIMIT_BYTES = 56 * 1024 * 1024
NEG_BIG = -1e30
ATT_SCALE = 1.0 / math.sqrt(FOX_HEAD_DIM)
SSM_Q_PROMPT = 256


def _cp(*sem):
    return pltpu.CompilerParams(dimension_semantics=sem, vmem_limit_bytes=V7X_VMEM_LIMIT_BYTES)


def _silu(x):
    return x * jax.nn.sigmoid(x)


def _rmsnorm_body(x_ref, g_ref, o_ref):
    x = x_ref[...]
    ms = jnp.mean(x * x, axis=-1, keepdims=True)
    o_ref[...] = (x * lax.rsqrt(ms + EPS) * g_ref[...]).astype(o_ref.dtype)


def rmsnorm_bf16(x, g, tm=384):
    m, d = x.shape
    return pl.pallas_call(
        _rmsnorm_body,
        grid=(m // tm,),
        in_specs=[pl.BlockSpec((tm, d), lambda i: (i, 0)),
                  pl.BlockSpec((1, d), lambda i: (0, 0))],
        out_specs=pl.BlockSpec((tm, d), lambda i: (i, 0)),
        out_shape=jax.ShapeDtypeStruct((m, d), BF16),
        compiler_params=_cp("parallel"),
        name="rmsnorm",
    )(x, g.reshape(1, d))


def _mm_body(a_ref, w_ref, *refs, nk, scale, has_res):
    if has_res:
        x_ref, o_ref, *scr = refs
    else:
        x_ref = None
        o_ref, *scr = refs
    d = jnp.dot(a_ref[...], w_ref[...], preferred_element_type=F32)

    def finish(acc):
        if has_res:
            acc = x_ref[...] + scale * acc
        o_ref[...] = acc.astype(o_ref.dtype)

    if nk == 1:
        finish(d)
    else:
        acc_ref = scr[0]
        k = pl.program_id(2)

        @pl.when(k == 0)
        def _():
            acc_ref[...] = d

        @pl.when(jnp.logical_and(k > 0, k < nk - 1))
        def _():
            acc_ref[...] += d

        @pl.when(k == nk - 1)
        def _():
            finish(acc_ref[...] + d)


def matmul(a, w, *, tm, tn, nk=1, layer=None, res=None, scale=1.0, out_dtype=F32, name="matmul"):
    m, kdim = a.shape
    n = w.shape[-1]
    tk = kdim // nk
    if layer is None:
        w_spec = pl.BlockSpec((tk, tn), lambda i, j, k: (k, j))
    else:
        w_spec = pl.BlockSpec((None, tk, tn), lambda i, j, k: (layer, k, j))
    in_specs = [pl.BlockSpec((tm, tk), lambda i, j, k: (i, k)), w_spec]
    args = [a, w]
    if res is not None:
        in_specs.append(pl.BlockSpec((tm, tn), lambda i, j, k: (i, j)))
        args.append(res)
    scratch = [pltpu.VMEM((tm, tn), F32)] if nk > 1 else []
    return pl.pallas_call(
        functools.partial(_mm_body, nk=nk, scale=scale, has_res=res is not None),
        grid=(m // tm, n // tn, nk),
        in_specs=in_specs,
        out_specs=pl.BlockSpec((tm, tn), lambda i, j, k: (i, j)),
        out_shape=jax.ShapeDtypeStruct((m, n), out_dtype),
        scratch_shapes=scratch,
        compiler_params=_cp("parallel", "parallel", "arbitrary"),
        name=name,
    )(*args)


LANES = 128
SUBLANES = 8
NT_DIMS = (((1,), (1,)), ((), ()))


def _wt_rows_spec(tn, kdim, layer, row0, step_of):
    if row0 % tn == 0:
        return pl.BlockSpec((None, tn, kdim), lambda *g: (layer, row0 // tn + step_of(*g), 0))
    assert row0 % SUBLANES == 0 and tn % SUBLANES == 0

    def index_map(*g):
        row = row0 + tn * step_of(*g)
        if not isinstance(row, int):
            row = pl.multiple_of(row, SUBLANES)
        return layer, row, 0

    return pl.BlockSpec((None, pl.Element(tn), pl.Element(kdim)), index_map)


def _mm_wst_body(a_ref, w_ref, o_ref, wb_ref):
    @pl.when(pl.program_id(1) == 0)
    def _():
        wb_ref[...] = w_ref[...].astype(BF16)

    o_ref[...] = lax.dot_general(a_ref[...], wb_ref[...], NT_DIMS,
                                 preferred_element_type=F32).astype(o_ref.dtype)


def matmul_wst(a, wt_stack, layer, *, row0, n, tm, tn, out_dtype=F32, name="matmul_wst"):
    m, kdim = a.shape
    return pl.pallas_call(
        _mm_wst_body,
        grid=(n // tn, m // tm),
        in_specs=[pl.BlockSpec((tm, kdim), lambda j, i: (i, 0)),
                  _wt_rows_spec(tn, kdim, layer, row0, lambda j, i: j)],
        out_specs=pl.BlockSpec((tm, tn), lambda j, i: (i, j)),
        out_shape=jax.ShapeDtypeStruct((m, n), out_dtype),
        scratch_shapes=[pltpu.VMEM((tn, kdim), BF16)],
        compiler_params=_cp("arbitrary", "arbitrary"),
        name=name,
    )(a, wt_stack)


def _cast_rows_body(w_ref, o_ref):
    o_ref[...] = w_ref[...].astype(o_ref.dtype)


def cast_rows(wt_stack, layer, *, row0, n, tr=512):
    kdim = wt_stack.shape[2]
    return pl.pallas_call(
        _cast_rows_body,
        grid=(n // tr,),
        in_specs=[_wt_rows_spec(tr, kdim, layer, row0, lambda r: r)],
        out_specs=pl.BlockSpec((tr, kdim), lambda r: (r, 0)),
        out_shape=jax.ShapeDtypeStruct((n, kdim), BF16),
        compiler_params=_cp("parallel"),
        name="cast_rows",
    )(wt_stack)


def _in_proj_s_body(h_ref, wf_ref, wdt_ref, o_ref):
    pad = jnp.zeros((LANES - FOX_HEADS - SSM_HEADS, wf_ref.shape[1]), F32)
    w = jnp.concatenate([wf_ref[...], wdt_ref[...], pad], axis=0).astype(BF16)
    o_ref[...] = lax.dot_general(h_ref[...], w, NT_DIMS, preferred_element_type=F32)


def in_proj_small(h, wt_stack, layer, tm=1056):
    m, kdim = h.shape
    return pl.pallas_call(
        _in_proj_s_body,
        grid=(m // tm,),
        in_specs=[pl.BlockSpec((tm, kdim), lambda i: (i, 0)),
                  _wt_rows_spec(FOX_HEADS, kdim, layer, OFF_F, lambda i: 0),
                  _wt_rows_spec(SSM_HEADS, kdim, layer, OFF_DT, lambda i: 0)],
        out_specs=pl.BlockSpec((tm, LANES), lambda i: (i, 0)),
        out_shape=jax.ShapeDtypeStruct((m, LANES), F32),
        compiler_params=_cp("parallel"),
        name="in_proj_s",
    )(h, wt_stack, wt_stack)


def _mm_ws_body(a_ref, w_ref, *refs, scale, has_res):
    if has_res:
        x_ref, o_ref, wb_ref = refs
    else:
        x_ref = None
        o_ref, wb_ref = refs

    @pl.when(pl.program_id(1) == 0)
    def _():
        wb_ref[...] = w_ref[...].astype(BF16)

    d = jnp.dot(a_ref[...], wb_ref[...], preferred_element_type=F32)
    if has_res:
        d = x_ref[...] + scale * d
    o_ref[...] = d.astype(o_ref.dtype)


def matmul_ws(a, w_stack, layer, *, n, tm, tn, col0=0, res=None, scale=1.0, out_dtype=F32, name="matmul_ws"):
    m, kdim = a.shape
    cb0 = col0 // tn
    in_specs = [pl.BlockSpec((tm, kdim), lambda j, i: (i, 0)),
                pl.BlockSpec((None, kdim, tn), lambda j, i: (layer, 0, cb0 + j))]
    args = [a, w_stack]
    if res is not None:
        in_specs.append(pl.BlockSpec((tm, tn), lambda j, i: (i, j)))
        args.append(res)
    return pl.pallas_call(
        functools.partial(_mm_ws_body, scale=scale, has_res=res is not None),
        grid=(n // tn, m // tm),
        in_specs=in_specs,
        out_specs=pl.BlockSpec((tm, tn), lambda j, i: (i, j)),
        out_shape=jax.ShapeDtypeStruct((m, n), out_dtype),
        scratch_shapes=[pltpu.VMEM((kdim, tn), BF16)],
        compiler_params=_cp("arbitrary", "arbitrary"),
        name=name,
    )(*args)


def _ffn_up_body(h_ref, wg_ref, wu_ref, o_ref, wgb_ref, wub_ref):
    @pl.when(pl.program_id(1) == 0)
    def _():
        wgb_ref[...] = wg_ref[...].astype(BF16)
        wub_ref[...] = wu_ref[...].astype(BF16)

    h = h_ref[...]
    g = jnp.dot(h, wgb_ref[...], preferred_element_type=F32)
    u = jnp.dot(h, wub_ref[...], preferred_element_type=F32)
    o_ref[...] = (_silu(g) * u).astype(o_ref.dtype)


def ffn_up(h, wg_stack, wu_stack, layer, tm=1056, tn=256):
    m, d = h.shape
    n = wg_stack.shape[2]
    w_spec = pl.BlockSpec((None, d, tn), lambda j, i: (layer, 0, j))
    return pl.pallas_call(
        _ffn_up_body,
        grid=(n // tn, m // tm),
        in_specs=[pl.BlockSpec((tm, d), lambda j, i: (i, 0)), w_spec, w_spec],
        out_specs=pl.BlockSpec((tm, tn), lambda j, i: (i, j)),
        out_shape=jax.ShapeDtypeStruct((m, n), BF16),
        scratch_shapes=[pltpu.VMEM((d, tn), BF16), pltpu.VMEM((d, tn), BF16)],
        compiler_params=_cp("arbitrary", "arbitrary"),
        name="ffn_up",
    )(h, wg_stack, wu_stack)


def ffn(x, norm_g, wg_stack, wu_stack, wd_stack_bf16, layer):
    h = rmsnorm_bf16(x, norm_g)
    g = ffn_up(h, wg_stack, wu_stack, layer)
    return matmul(g, wd_stack_bf16, layer=layer, tm=1056, tn=512, nk=2, res=x, scale=0.5, name="ffn_down")


def _merge_body(h_ref, ya_ref, yb_ref, yc_ref, yd_ref, g0_ref, g1_ref, g2_ref, g3_ref,
                wa_ref, wb_ref, wc_ref, wd_ref, bias_ref, o_ref):
    h = h_ref[...]
    acc = None
    branches = ((ya_ref, g0_ref, wa_ref), (yb_ref, g1_ref, wb_ref),
                (yc_ref, g2_ref, wc_ref), (yd_ref, g3_ref, wd_ref))
    for b, (y_ref, g_ref, w_ref) in enumerate(branches):
        gate = jax.nn.sigmoid(lax.dot_general(h, g_ref[...], NT_DIMS, preferred_element_type=F32)
                              + bias_ref[b:b + 1, :])
        t = gate * jnp.dot(y_ref[...], w_ref[...], preferred_element_type=F32)
        acc = t if acc is None else acc + t
    o_ref[...] = acc.astype(o_ref.dtype)


def merge(h, ys, wt_gate, w_ups, layer, gate_bias, tm=528, tn=256):
    m, d = h.shape
    nb = D_MODEL // tn
    y_spec = pl.BlockSpec((tm, CONV_DIM), lambda i, j: (i, 0))
    in_specs = [pl.BlockSpec((tm, d), lambda i, j: (i, 0))] + [y_spec] * 4
    for b in range(N_BRANCH):
        in_specs.append(pl.BlockSpec((tn, d), lambda i, j, b=b: (b * nb + j, 0)))
    in_specs += [pl.BlockSpec((None, CONV_DIM, tn), lambda i, j: (layer, 0, j))] * 4
    in_specs.append(pl.BlockSpec((N_BRANCH, tn), lambda i, j: (0, j)))
    return pl.pallas_call(
        _merge_body,
        grid=(m // tm, nb),
        in_specs=in_specs,
        out_specs=pl.BlockSpec((tm, tn), lambda i, j: (i, j)),
        out_shape=jax.ShapeDtypeStruct((m, D_MODEL), BF16),
        compiler_params=_cp("parallel", "parallel"),
        name="merge",
    )(h, *ys, wt_gate, wt_gate, wt_gate, wt_gate, *w_ups, gate_bias)


def _alias_rows(ybuf, in_specs, args):
    if ybuf is None:
        return {}
    in_specs.append(pl.BlockSpec(memory_space=pl.ANY))
    args.append(ybuf)
    return {len(args) - 1: 0}


def _conv_a_body(xa_ref, gb_ref, gc_ref, buf_ref, w_ref, *refs, L):
    y_ref, nb_ref, scr_ref = refs[-3:]
    t = gc_ref[...] * xa_ref[...]
    scr_ref[pl.ds(8, L), :] = t
    scr_ref[pl.ds(6, 2), :] = buf_ref[0]
    w = w_ref[...]
    y = scr_ref[pl.ds(6, L), :] * w[0:1] + scr_ref[pl.ds(7, L), :] * w[1:2] + t * w[2:3]
    y_ref[...] = (gb_ref[...] * y).astype(y_ref.dtype)
    nb_ref[0] = scr_ref[pl.ds(6 + L, 2), :]


def conv_a(za, buf, w, *, L, nseq, row0, ybuf=None, cb=256):
    ncb = CONV_DIM // cb
    rb = row0 // L

    def zspec(grp):
        return pl.BlockSpec((L, cb), lambda s, c: (rb + s, grp * ncb + c))

    in_specs = [zspec(0), zspec(1), zspec(2),
                pl.BlockSpec((1, CONV_WIDTH - 1, cb), lambda s, c: (s, 0, c)),
                pl.BlockSpec((CONV_WIDTH, cb), lambda s, c: (0, c))]
    args = [za, za, za, buf, w]
    aliases = _alias_rows(ybuf, in_specs, args)
    return pl.pallas_call(
        functools.partial(_conv_a_body, L=L),
        grid=(nseq, ncb),
        in_specs=in_specs,
        out_specs=[pl.BlockSpec((L, cb), lambda s, c: (rb + s, c)),
                   pl.BlockSpec((1, CONV_WIDTH - 1, cb), lambda s, c: (s, 0, c))],
        out_shape=[jax.ShapeDtypeStruct((M_ALL, CONV_DIM), BF16),
                   jax.ShapeDtypeStruct((nseq, CONV_WIDTH - 1, CONV_DIM), F32)],
        scratch_shapes=[pltpu.VMEM((L + 8, cb), F32)],
        input_output_aliases=aliases,
        compiler_params=_cp("parallel", "parallel"),
        name="conv_a",
    )(*args)


def _qknorm_body(q_ref, k_ref, v_ref, gq_ref, gk_ref, *refs):
    qn_ref, knb_ref, vb_ref, kn_ref, vf_ref = refs[-5:]
    gq = gq_ref[...]
    gk = gk_ref[...]
    for h in range(FOX_HEADS):
        sl = slice(h * FOX_HEAD_DIM, (h + 1) * FOX_HEAD_DIM)
        q = q_ref[:, sl]
        qn = q * lax.rsqrt(jnp.mean(q * q, axis=-1, keepdims=True) + EPS) * gq
        qn_ref[:, sl] = qn.astype(qn_ref.dtype)
        k = k_ref[:, sl]
        kn = k * lax.rsqrt(jnp.mean(k * k, axis=-1, keepdims=True) + EPS) * gk
        kn_ref[:, sl] = kn
        knb_ref[:, sl] = kn.astype(knb_ref.dtype)
    v = v_ref[...]
    vf_ref[...] = v
    vb_ref[...] = v.astype(vb_ref.dtype)


def qk_norm(za, gq, gk, *, rows, row0, tm, layer=None, kn_stack=None, v_stack=None):
    rb = row0 // tm

    def zspec(grp):
        return pl.BlockSpec((tm, FOX_DIM), lambda i: (rb + i, grp))

    ospec = pl.BlockSpec((tm, FOX_DIM), lambda i: (i, 0))
    gspec = pl.BlockSpec((1, FOX_HEAD_DIM), lambda i: (0, 0))
    in_specs = [zspec(3), zspec(4), zspec(5), gspec, gspec]
    args = [za, za, za, gq.reshape(1, -1), gk.reshape(1, -1)]
    aliases = {}
    if layer is None:
        f32_spec = ospec
        f32_shape = jax.ShapeDtypeStruct((rows, FOX_DIM), F32)
    else:
        f32_spec = pl.BlockSpec((None, tm, FOX_DIM), lambda i: (layer, i, 0))
        f32_shape = jax.ShapeDtypeStruct((DEPTH, rows, FOX_DIM), F32)
        if kn_stack is not None:
            in_specs += [pl.BlockSpec(memory_space=pl.ANY)] * 2
            args += [kn_stack, v_stack]
            aliases = {5: 3, 6: 4}
    bf_shape = jax.ShapeDtypeStruct((rows, FOX_DIM), BF16)
    return pl.pallas_call(
        _qknorm_body,
        grid=(rows // tm,),
        in_specs=in_specs,
        out_specs=[ospec, ospec, ospec, f32_spec, f32_spec],
        out_shape=[bf_shape, bf_shape, bf_shape, f32_shape, f32_shape],
        input_output_aliases=aliases,
        compiler_params=_cp("parallel"),
        name="qk_norm",
    )(*args)


def _flash_body(q_ref, k_ref, v_ref, fq_ref, fkt_ref, o_ref, m_ref, l_ref, acc_ref, *, tq, tk, nk):
    qi = pl.program_id(1)
    ki = pl.program_id(2)
    lanes = FOX_HEAD_DIM

    @pl.when(ki == 0)
    def _():
        m_ref[...] = jnp.full(m_ref.shape, NEG_BIG, F32)
        l_ref[...] = jnp.zeros(l_ref.shape, F32)
        acc_ref[...] = jnp.zeros(acc_ref.shape, F32)

    def step(on_diagonal):
        fq = fq_ref[...]
        fkt = fkt_ref[...]
        if on_diagonal:
            mask = (lax.broadcasted_iota(jnp.int32, (tq, tk), 1)
                    <= lax.broadcasted_iota(jnp.int32, (tq, tk), 0))
        for h in range(FOX_HEADS):
            sl = slice(h * FOX_HEAD_DIM, (h + 1) * FOX_HEAD_DIM)
            s = lax.dot_general(q_ref[:, sl], k_ref[:, sl], (((1,), (1,)), ((), ())),
                                preferred_element_type=F32) * ATT_SCALE
            s = s + (fq[:, h:h + 1] - fkt[h:h + 1, :])
            if on_diagonal:
                s = jnp.where(mask, s, NEG_BIG)
            m_prev = m_ref[h]
            m_new = jnp.maximum(m_prev, jnp.max(s, axis=-1, keepdims=True))
            alpha = jnp.exp(m_prev - m_new)
            p = jnp.exp(s - jnp.concatenate([m_new] * (tk // lanes), axis=1))
            l_ref[h] = alpha * l_ref[h] + jnp.sum(p, axis=-1, keepdims=True)
            acc_ref[:, sl] = alpha * acc_ref[:, sl] + jnp.dot(
                p.astype(BF16), v_ref[:, sl], preferred_element_type=F32)
            m_ref[h] = m_new

    @pl.when(ki < qi)
    def _():
        step(False)

    @pl.when(ki == qi)
    def _():
        step(True)

    @pl.when(ki == nk - 1)
    def _():
        for h in range(FOX_HEADS):
            sl = slice(h * FOX_HEAD_DIM, (h + 1) * FOX_HEAD_DIM)
            o_ref[:, sl] = (acc_ref[:, sl] / l_ref[h]).astype(o_ref.dtype)


def fox_prompt(qn, knb, vb, f_cum, f_cum_t, t=512):
    nb = SEQ // t
    kv_spec = pl.BlockSpec((t, FOX_DIM), lambda b, qi, ki: (b * nb + jnp.minimum(ki, qi), 0))
    return pl.pallas_call(
        functools.partial(_flash_body, tq=t, tk=t, nk=nb),
        grid=(BATCH, nb, nb),
        in_specs=[pl.BlockSpec((t, FOX_DIM), lambda b, qi, ki: (b * nb + qi, 0)),
                  kv_spec, kv_spec,
                  pl.BlockSpec((t, FOX_HEADS), lambda b, qi, ki: (b * nb + qi, 0)),
                  pl.BlockSpec((FOX_HEADS, t), lambda b, qi, ki: (0, b * nb + jnp.minimum(ki, qi)))],
        out_specs=pl.BlockSpec((t, FOX_DIM), lambda b, qi, ki: (b * nb + qi, 0)),
        out_shape=jax.ShapeDtypeStruct((M_ALL, FOX_DIM), BF16),
        scratch_shapes=[pltpu.VMEM((FOX_HEADS, t, FOX_HEAD_DIM), F32),
                        pltpu.VMEM((FOX_HEADS, t, FOX_HEAD_DIM), F32),
                        pltpu.VMEM((t, FOX_DIM), F32)],
        compiler_params=_cp("parallel", "parallel", "arbitrary"),
        name="fox_prompt",
    )(qn, knb, vb, f_cum, f_cum_t)


def _fox_sample_body(q_ref, kc_ref, vc_ref, kn_ref, vn_ref, bc_ref, bn_ref, *refs):
    o_ref = refs[-1]
    rows = FOX_HEADS * DEC_SEQ
    q = q_ref[...]
    qrep = jnp.concatenate([q] * FOX_HEADS, axis=0)
    r_head = lax.broadcasted_iota(jnp.int32, (rows, FOX_DIM), 0) // DEC_SEQ
    c_head = lax.broadcasted_iota(jnp.int32, (rows, FOX_DIM), 1) // FOX_HEAD_DIM
    own = r_head == c_head
    qbd = jnp.where(own, qrep, jnp.zeros_like(qrep))
    nt = (((1,), (1,)), ((), ()))
    pad = jnp.zeros((rows - DEC_SEQ, FOX_DIM), BF16)
    kn = jnp.concatenate([kn_ref[...], pad], axis=0)
    vn = jnp.concatenate([vn_ref[...], pad], axis=0)

    def heads_on_lanes(c_ref):
        return jnp.concatenate(
            [c_ref[0, pl.ds(h, PAST_LEN, stride=FOX_HEADS), :].astype(BF16) for h in range(FOX_HEADS)], axis=1)

    kc = heads_on_lanes(kc_ref)
    vc = heads_on_lanes(vc_ref)
    s_c = lax.dot_general(qbd, kc, nt, preferred_element_type=F32) * ATT_SCALE + bc_ref[0]
    s_n = lax.dot_general(qbd, kn, nt, preferred_element_type=F32) * ATT_SCALE + bn_ref[0]
    m = jnp.maximum(jnp.max(s_c, axis=-1, keepdims=True), jnp.max(s_n, axis=-1, keepdims=True))
    p_c = jnp.exp(s_c - m)
    p_n = jnp.exp(s_n - m)
    inv = 1.0 / (jnp.sum(p_c, axis=-1, keepdims=True) + jnp.sum(p_n, axis=-1, keepdims=True))
    o_full = (jnp.dot((p_c * inv).astype(BF16), vc, preferred_element_type=F32)
              + jnp.dot((p_n * inv).astype(BF16), vn, preferred_element_type=F32))
    o_full = jnp.where(own, o_full, 0.0)
    o = o_full[0:DEC_SEQ]
    for h in range(1, FOX_HEADS):
        o = o + o_full[h * DEC_SEQ:(h + 1) * DEC_SEQ]
    o_ref[...] = o.astype(o_ref.dtype)


def fox_sample(qn_s, knb_s, vb_s, cache_k, cache_v, layer, bias_c, bias_n, ybuf):
    rows = FOX_HEADS * DEC_SEQ
    new_spec = pl.BlockSpec((DEC_SEQ, FOX_DIM), lambda b: (b, 0))
    cache_spec = pl.BlockSpec((None, 1, PAST_LEN * FOX_HEADS, FOX_HEAD_DIM), lambda b: (layer, b, 0, 0))
    in_specs = [new_spec, cache_spec, cache_spec, new_spec, new_spec,
                pl.BlockSpec((1, rows, PAST_LEN), lambda b: (b, 0, 0)),
                pl.BlockSpec((1, rows, rows), lambda b: (b, 0, 0))]
    args = [qn_s, cache_k, cache_v, knb_s, vb_s, bias_c, bias_n]
    aliases = _alias_rows(ybuf, in_specs, args)
    return pl.pallas_call(
        _fox_sample_body,
        grid=(DEC_BATCH,),
        in_specs=in_specs,
        out_specs=pl.BlockSpec((DEC_SEQ, FOX_DIM), lambda b: (M_P // DEC_SEQ + b, 0)),
        out_shape=jax.ShapeDtypeStruct((M_ALL, FOX_DIM), BF16),
        input_output_aliases=aliases,
        compiler_params=_cp("parallel"),
        name="fox_sample",
    )(*args)


def _gmlp_body(u_ref, vg_ref, gn_ref, w_ref, b_ref, *refs, emit_vn):
    if emit_vn:
        y_ref, vn_ref = refs[-2:]
    else:
        y_ref, vn_ref = refs[-1], None
    v = jax.nn.gelu(vg_ref[...])
    vn = v * lax.rsqrt(jnp.mean(v * v, axis=-1, keepdims=True) + EPS) * gn_ref[...]
    if emit_vn:
        vn_ref[...] = vn
    vb = vn.astype(BF16)
    bias = b_ref[...]
    for g in range(GMLP_GROUPS):
        sl = slice(g * GMLP_GROUP_DIM, (g + 1) * GMLP_GROUP_DIM)
        s = jnp.dot(w_ref[g], vb[:, sl], preferred_element_type=F32) + bias[:, g:g + 1]
        y_ref[:, sl] = (jax.nn.gelu(u_ref[:, sl]) * s).astype(y_ref.dtype)


def gmlp(zc, gn, w_eff, b_eff, *, R, nchunk, row0, emit_vn, ybuf=None):
    rb = row0 // R
    out_specs = [pl.BlockSpec((R, GMLP_DIM), lambda c: (rb + c, 0))]
    out_shape = [jax.ShapeDtypeStruct((M_ALL, GMLP_DIM), BF16)]
    if emit_vn:
        out_specs.append(pl.BlockSpec((R, GMLP_DIM), lambda c: (c, 0)))
        out_shape.append(jax.ShapeDtypeStruct((nchunk * R, GMLP_DIM), F32))
    in_specs = [pl.BlockSpec((R, GMLP_DIM), lambda c: (rb + c, 0)),
                pl.BlockSpec((R, GMLP_DIM), lambda c: (rb + c, 1)),
                pl.BlockSpec((1, GMLP_DIM), lambda c: (0, 0)),
                pl.BlockSpec((GMLP_GROUPS, R, R), lambda c: (0, 0, 0)),
                pl.BlockSpec((R, GMLP_GROUPS), lambda c: (0, 0))]
    args = [zc, zc, gn.reshape(1, -1), w_eff, b_eff]
    aliases = _alias_rows(ybuf, in_specs, args)
    return pl.pallas_call(
        functools.partial(_gmlp_body, emit_vn=emit_vn),
        grid=(nchunk,),
        in_specs=in_specs,
        out_specs=out_specs,
        out_shape=out_shape,
        input_output_aliases=aliases,
        compiler_params=_cp("parallel"),
        name="gmlp",
    )(*args)


def _ssm_body(z_ref, xbc_ref, dt_ref, adtt_ref, a_ref, cbuf_ref, h0_ref, cw_ref, cb_ref, dfull_ref, nw_ref,
              *refs, Q):
    y_ref, ncb_ref, hs_out_ref, scr_ref, hs_ref = refs[-5:]
    c = pl.program_id(1)
    nc = pl.num_programs(1)
    tail = SSM_CONV - 1
    lo = 8 - tail

    @pl.when(c == 0)
    def _():
        scr_ref[pl.ds(lo, tail), :] = cbuf_ref[0]
        hs_ref[...] = h0_ref[0]

    xbc = xbc_ref[...]
    scr_ref[pl.ds(8, Q), :] = xbc
    cw = cw_ref[...]
    conv = xbc * cw[tail:tail + 1] + cb_ref[...]
    for i in range(tail):
        conv = conv + scr_ref[pl.ds(lo + i, Q), :] * cw[i:i + 1]
    xc = _silu(conv)
    new_tail = scr_ref[pl.ds(lo + Q, tail), :]
    ncb_ref[0] = new_tail
    scr_ref[pl.ds(lo, tail), :] = new_tail

    xs = xc[:, :SSM_DIM]
    gw = SSM_STATE
    dt = dt_ref[...]
    adt = dt * a_ref[...]
    ii = lax.broadcasted_iota(jnp.int32, (Q, Q), 0)
    jj = lax.broadcasted_iota(jnp.int32, (Q, Q), 1)
    tri = ii >= jj
    acum = jnp.dot(tri.astype(F32), adt, precision=lax.Precision.HIGHEST,
                   preferred_element_type=F32)
    acum_t = jnp.dot(adtt_ref[0], (ii <= jj).astype(F32), precision=lax.Precision.HIGHEST,
                     preferred_element_type=F32)
    a_last = acum[Q - 1:Q, :]
    lane_lo = lax.broadcasted_iota(jnp.int32, (Q, 2 * SSM_HEAD_DIM), 1) < SSM_HEAD_DIM
    row_lo = lax.broadcasted_iota(jnp.int32, (2 * SSM_HEAD_DIM, SSM_STATE), 0) < SSM_HEAD_DIM
    nt = (((1,), (1,)), ((), ()))
    tn = (((0,), (0,)), ((), ()))
    dfull = dfull_ref[...]
    ys = []
    s_g = None
    for pr in range(SSM_HEADS // 2):
        g = (2 * pr) // (SSM_HEADS // SSM_GROUPS)
        b_g = xc[:, SSM_DIM + g * gw:SSM_DIM + (g + 1) * gw]
        c_g = xc[:, SSM_DIM + SSM_GROUPS * gw + g * gw:SSM_DIM + SSM_GROUPS * gw + (g + 1) * gw]
        c_gb = c_g.astype(BF16)
        if (2 * pr) % (SSM_HEADS // SSM_GROUPS) == 0:
            s_g = lax.dot_general(c_gb, b_g.astype(BF16), nt, preferred_element_type=F32)
        h0i, h1i = 2 * pr, 2 * pr + 1
        sl = slice(pr * 2 * SSM_HEAD_DIM, (pr + 1) * 2 * SSM_HEAD_DIM)
        xs_p = xs[:, sl]
        a0 = acum[:, h0i:h0i + 1]
        a1 = acum[:, h1i:h1i + 1]
        l0 = jnp.exp(jnp.where(tri, a0 - acum_t[h0i:h0i + 1, :], -jnp.inf))
        l1 = jnp.exp(jnp.where(tri, a1 - acum_t[h1i:h1i + 1, :], -jnp.inf))
        xdt = (xs_p * jnp.where(lane_lo, dt[:, h0i:h0i + 1], dt[:, h1i:h1i + 1])).astype(BF16)
        y = jnp.where(lane_lo,
                      jnp.dot((s_g * l0).astype(BF16), xdt, preferred_element_type=F32),
                      jnp.dot((s_g * l1).astype(BF16), xdt, preferred_element_type=F32))
        hs_p = hs_ref[pr]
        y_off = lax.dot_general(c_gb, hs_p.astype(BF16), nt, preferred_element_type=F32)
        y = y + y_off * jnp.exp(jnp.where(lane_lo, a0, a1)) + dfull[:, sl] * xs_p
        ys.append(y)
        al0 = a_last[:, h0i:h0i + 1]
        al1 = a_last[:, h1i:h1i + 1]
        bw0 = (b_g * jnp.exp(al0 - a0)).astype(BF16)
        bw1 = (b_g * jnp.exp(al1 - a1)).astype(BF16)
        r0 = lax.dot_general(xdt, bw0, tn, preferred_element_type=F32)
        r1 = lax.dot_general(xdt, bw1, tn, preferred_element_type=F32)
        decay = jnp.where(row_lo, jnp.exp(al0), jnp.exp(al1))
        hs_ref[pr] = hs_p * decay + jnp.where(row_lo, r0, r1)

    z = z_ref[...]
    y_all = jnp.concatenate(ys, axis=1) * _silu(z)
    gsz = SSM_DIM // SSM_GROUPS
    nw = nw_ref[...]
    for g in range(SSM_GROUPS):
        sl = slice(g * gsz, (g + 1) * gsz)
        seg = y_all[:, sl]
        y_ref[:, sl] = (seg * lax.rsqrt(jnp.mean(seg * seg, axis=-1, keepdims=True) + EPS)
                        * nw[:, sl]).astype(y_ref.dtype)

    @pl.when(c == nc - 1)
    def _():
        hs_out_ref[0] = hs_ref[...]


def ssm(zc, dt, adt_t, adt_t_index, a_row, cbuf, h0, cw, cb, dfull, nw, *, Q, nseq, L, row0, ybuf=None):
    nchunk = L // Q
    rb = row0 // Q
    npair = SSM_HEADS // 2
    in_specs = [pl.BlockSpec((Q, SSM_DIM), lambda s, c: (rb + s * nchunk + c, 2)),
                pl.BlockSpec((Q, SSM_XBC), lambda s, c: (rb + s * nchunk + c, 2)),
                pl.BlockSpec((Q, SSM_HEADS), lambda s, c: (s * nchunk + c, 0)),
                pl.BlockSpec((1, SSM_HEADS, Q), adt_t_index),
                pl.BlockSpec((1, SSM_HEADS), lambda s, c: (0, 0)),
                pl.BlockSpec((1, SSM_CONV - 1, SSM_XBC), lambda s, c: (s, 0, 0)),
                pl.BlockSpec((1, npair, 2 * SSM_HEAD_DIM, SSM_STATE), lambda s, c: (s, 0, 0, 0)),
                pl.BlockSpec((SSM_CONV, SSM_XBC), lambda s, c: (0, 0)),
                pl.BlockSpec((1, SSM_XBC), lambda s, c: (0, 0)),
                pl.BlockSpec((1, SSM_DIM), lambda s, c: (0, 0)),
                pl.BlockSpec((1, SSM_DIM), lambda s, c: (0, 0))]
    args = [zc, zc, dt, adt_t, a_row, cbuf, h0, cw, cb, dfull, nw]
    aliases = _alias_rows(ybuf, in_specs, args)
    return pl.pallas_call(
        functools.partial(_ssm_body, Q=Q),
        grid=(nseq, nchunk),
        in_specs=in_specs,
        out_specs=[pl.BlockSpec((Q, SSM_DIM), lambda s, c: (rb + s * nchunk + c, 0)),
                   pl.BlockSpec((1, SSM_CONV - 1, SSM_XBC), lambda s, c: (s, 0, 0)),
                   pl.BlockSpec((1, npair, 2 * SSM_HEAD_DIM, SSM_STATE), lambda s, c: (s, 0, 0, 0))],
        out_shape=[jax.ShapeDtypeStruct((M_ALL, SSM_DIM), BF16),
                   jax.ShapeDtypeStruct((nseq, SSM_CONV - 1, SSM_XBC), F32),
                   jax.ShapeDtypeStruct((nseq, npair, 2 * SSM_HEAD_DIM, SSM_STATE), F32)],
        scratch_shapes=[pltpu.VMEM((Q + 8, SSM_XBC), F32),
                        pltpu.VMEM((npair, 2 * SSM_HEAD_DIM, SSM_STATE), F32)],
        input_output_aliases=aliases,
        compiler_params=_cp("parallel", "arbitrary"),
        name="ssm",
    )(*args)


def _token_mix(x, l, p, st, kn_stack, v_stack):
    h = rmsnorm_bf16(x, p['mix_norm'][l])
    wt_in = p['wt_in']
    za = matmul_wst(h, wt_in, l, row0=0, n=OFF_F, tm=1056, tn=512, name="in_proj_a")
    zc = matmul_wst(h, wt_in, l, row0=OFF_U, n=N_ZC, tm=1056, tn=512, name="in_proj_c")
    zs = in_proj_small(h, wt_in, l)
    zs_t = zs.T
    logf_t = jax.nn.log_sigmoid(zs_t[:FOX_HEADS] + p['fox_f_bias'][l][:, None])
    dt_t = jax.nn.softplus(zs_t[FOX_HEADS:FOX_HEADS + SSM_HEADS] + p['ssm_dt_bias'][l][:, None])

    conv_w = p['conv_a_w'][l]
    ya, nca_p = conv_a(za, jnp.zeros((BATCH, CONV_WIDTH - 1, CONV_DIM), F32), conv_w,
                       L=SEQ, nseq=BATCH, row0=0)
    ya, nca_s = conv_a(za, st['conv_a'][l], conv_w, L=DEC_SEQ, nseq=DEC_BATCH, row0=M_P, ybuf=ya)

    gq, gk = p['fox_q_norm'][l], p['fox_k_norm'][l]
    qn_p, knb_p, vb_p, kn_stack, v_stack = qk_norm(za, gq, gk, rows=M_P, row0=0, tm=512, layer=l,
                                                   kn_stack=kn_stack, v_stack=v_stack)
    qn_s, knb_s, vb_s, kn_s, v_s = qk_norm(za, gq, gk, rows=M_S, row0=M_P, tm=M_S)
    f_t_p = jnp.cumsum(logf_t[:, :M_P].reshape(FOX_HEADS, BATCH, SEQ), axis=2).reshape(FOX_HEADS, M_P)
    yb = fox_prompt(qn_p, knb_p, vb_p, f_t_p.T, f_t_p)
    logf_s_t = jnp.transpose(logf_t[:, M_P:].reshape(FOX_HEADS, DEC_BATCH, DEC_SEQ), (1, 0, 2))
    cache_logf_t = jnp.transpose(st['fox_logf'][l], (0, 2, 1))
    f_all_t = jnp.cumsum(jnp.concatenate([cache_logf_t, logf_s_t], axis=2), axis=2)
    bias = f_all_t[:, :, PAST_LEN:, None] - f_all_t[:, :, None, :]
    kpos = jnp.arange(PAST_LEN + DEC_SEQ)
    qpos = PAST_LEN + jnp.arange(DEC_SEQ)
    bias = jnp.where(kpos[None, :] <= qpos[:, None], bias, NEG_BIG)
    bias = bias.reshape(DEC_BATCH, FOX_HEADS * DEC_SEQ, PAST_LEN + DEC_SEQ)
    bias_n = jnp.pad(bias[:, :, PAST_LEN:], ((0, 0), (0, 0), (0, FOX_HEADS * DEC_SEQ - DEC_SEQ)),
                     constant_values=NEG_BIG)
    yb = fox_sample(qn_s, knb_s, vb_s,
                    st['fox_k'].reshape(DEPTH, DEC_BATCH, PAST_LEN * FOX_HEADS, FOX_HEAD_DIM),
                    st['fox_v'].reshape(DEPTH, DEC_BATCH, PAST_LEN * FOX_HEADS, FOX_HEAD_DIM),
                    l, bias[:, :, :PAST_LEN], bias_n, yb)

    ws = p['gmlp_ws'][l]
    bs = p['gmlp_bs'][l]
    idx = jnp.arange(GMLP_CHUNK)
    cmask = (idx[None, :] // CHUNK) <= (idx[:, None] // CHUNK)
    w_p = jnp.where(cmask[None], ws, 0.0).astype(BF16)
    yc = gmlp(zc, p['gmlp_v_norm'][l], w_p, bs.T, R=GMLP_CHUNK, nchunk=M_P // GMLP_CHUNK, row0=0,
              emit_vn=False)[0]
    eye = jnp.eye(DEC_BATCH, dtype=F32)
    w_s = jax.vmap(lambda w: jnp.kron(eye, w))(ws[:, :DEC_SEQ, :DEC_SEQ]).astype(BF16)
    b_s = jnp.tile(bs[:, :DEC_SEQ].T, (DEC_BATCH, 1))
    yc, vn_s = gmlp(zc, p['gmlp_v_norm'][l], w_s, b_s, R=M_S, nchunk=1, row0=M_P, emit_vn=True, ybuf=yc)

    a = -jnp.exp(p['ssm_a_log'][l])
    adt_t = dt_t * a[:, None]
    dt = dt_t.T
    qp = SSM_Q_PROMPT
    nchunk_p = SEQ // qp
    adt_t_p = adt_t[None, :, :M_P]
    adt_t_s = jnp.transpose(adt_t[:, M_P:].reshape(SSM_HEADS, DEC_BATCH, DEC_SEQ), (1, 0, 2))
    dfull = jnp.repeat(p['ssm_d'][l], SSM_HEAD_DIM).reshape(1, SSM_DIM)
    common = (p['ssm_conv_w'][l], p['ssm_conv_b'][l].reshape(1, -1), dfull, p['ssm_norm'][l].reshape(1, -1))
    npair = SSM_HEADS // 2
    yd, ncs_p, hs_p = ssm(zc, dt[:M_P], adt_t_p, lambda s, c: (0, 0, s * nchunk_p + c), a.reshape(1, -1),
                          jnp.zeros((BATCH, SSM_CONV - 1, SSM_XBC), F32),
                          jnp.zeros((BATCH, npair, 2 * SSM_HEAD_DIM, SSM_STATE), F32),
                          *common, Q=qp, nseq=BATCH, L=SEQ, row0=0)
    yd, ncs_s, hs_s = ssm(zc, dt[M_P:], adt_t_s, lambda s, c: (s, 0, 0), a.reshape(1, -1), st['conv_ssm'][l],
                          st['ssm'][l].reshape(DEC_BATCH, npair, 2 * SSM_HEAD_DIM, SSM_STATE),
                          *common, Q=DEC_SEQ, nseq=DEC_BATCH, L=DEC_SEQ, row0=M_P, ybuf=yd)

    merged = merge(h, [ya, yb, yc, yd], cast_rows(wt_in, l, row0=OFF_G, n=N_BRANCH * D_MODEL),
                   p['w_ups_bf16'], l, p['gate_bias'][l])
    x = matmul_ws(merged, p['w_out'], l, n=D_MODEL, tm=1056, tn=512, res=x, scale=1.0, name="out_proj")

    logf = logf_t.T
    hshape = (SSM_HEADS, SSM_HEAD_DIM, SSM_STATE)
    new_p = (nca_p, ncs_p, hs_p.reshape((BATCH,) + hshape), logf[:M_P].reshape(BATCH, SEQ, FOX_HEADS))
    new_s = (nca_s, ncs_s, hs_s.reshape((DEC_BATCH,) + hshape),
             kn_s.reshape(DEC_BATCH, DEC_SEQ, FOX_HEADS, FOX_HEAD_DIM),
             v_s.reshape(DEC_BATCH, DEC_SEQ, FOX_HEADS, FOX_HEAD_DIM),
             logf[M_P:].reshape(DEC_BATCH, DEC_SEQ, FOX_HEADS), vn_s.reshape(DEC_BATCH, DEC_SEQ, GMLP_DIM))
    return x, new_p, new_s, kn_stack, v_stack


def kernel(x_prompt, x_sample, cache_fox_k, cache_fox_v, cache_fox_logf, state_conv_a, state_conv_ssm, state_ssm, ffn1_norm, ffn1_w_gate, ffn1_w_up, ffn1_w_down, mix_norm, w_in, gate_bias, conv_a_w, w_up_a, fox_q_norm, fox_k_norm, fox_f_bias, w_up_b, gmlp_v_norm, gmlp_ws, gmlp_bs, w_up_c, ssm_conv_w, ssm_conv_b, ssm_dt_bias, ssm_a_log, ssm_d, ssm_norm, w_up_d, w_out, ffn2_norm, ffn2_w_gate, ffn2_w_up, ffn2_w_down):
    p = dict(mix_norm=mix_norm, w_in=w_in, gate_bias=gate_bias, conv_a_w=conv_a_w, w_up_a=w_up_a,
             fox_q_norm=fox_q_norm, fox_k_norm=fox_k_norm, fox_f_bias=fox_f_bias, w_up_b=w_up_b,
             gmlp_v_norm=gmlp_v_norm, gmlp_ws=gmlp_ws, gmlp_bs=gmlp_bs, w_up_c=w_up_c,
             ssm_conv_w=ssm_conv_w, ssm_conv_b=ssm_conv_b, ssm_dt_bias=ssm_dt_bias, ssm_a_log=ssm_a_log,
             ssm_d=ssm_d, ssm_norm=ssm_norm, w_up_d=w_up_d, w_out=w_out)
    st = dict(fox_k=cache_fox_k, fox_v=cache_fox_v, fox_logf=cache_fox_logf,
              conv_a=state_conv_a, conv_ssm=state_conv_ssm, ssm=state_ssm)
    x = jnp.concatenate([x_prompt.reshape(M_P, D_MODEL), x_sample.reshape(M_S, D_MODEL)], axis=0)
    p['w_ups_bf16'] = [w.astype(BF16) for w in (w_up_a, w_up_b, w_up_c, w_up_d)]
    p['wt_in'] = jnp.transpose(w_in, (0, 2, 1))
    wd1 = ffn1_w_down.astype(BF16)
    wd2 = ffn2_w_down.astype(BF16)
    st_p, st_s = [], []
    kn_stack = v_stack = None
    for l in range(DEPTH):
        x = ffn(x, ffn1_norm[l], ffn1_w_gate, ffn1_w_up, wd1, l)
        x, new_p, new_s, kn_stack, v_stack = _token_mix(x, l, p, st, kn_stack, v_stack)
        x = ffn(x, ffn2_norm[l], ffn2_w_gate, ffn2_w_up, wd2, l)
        st_p.append(new_p)
        st_s.append(new_s)

    def stack(states, i):
        return jnp.stack([s[i] for s in states])

    kv_shape = (DEPTH, BATCH, SEQ, FOX_HEADS, FOX_HEAD_DIM)
    return (x[:M_P].reshape(BATCH, SEQ, D_MODEL), x[M_P:].reshape(DEC_BATCH, DEC_SEQ, D_MODEL),
            kn_stack.reshape(kv_shape), v_stack.reshape(kv_shape),
            stack(st_p, 3), stack(st_p, 0), stack(st_p, 1), stack(st_p, 2),
            stack(st_s, 3), stack(st_s, 4), stack(st_s, 5), stack(st_s, 0), stack(st_s, 1), stack(st_s, 2),
            stack(st_s, 6))
```

```python
import functools
import math

import jax
import jax.numpy as jnp
from jax import lax
from jax.experimental import pallas as pl
from jax.experimental.pallas import tpu as pltpu

F32 = jnp.float32
BF16 = jnp.bfloat16

D_MODEL = 4096
BATCH = 4
SEQ = 2048
DEPTH = 4
DEC_BATCH = 16
DEC_SEQ = 16
PAST_LEN = 2048
CHUNK = 64
D_FF = 11008
EPS = 1e-6
N_BRANCH = 4
CONV_DIM = 1024
CONV_WIDTH = 3
FOX_HEADS = 8
FOX_HEAD_DIM = 128
FOX_DIM = FOX_HEADS * FOX_HEAD_DIM
GMLP_GROUPS = 8
GMLP_GROUP_DIM = 128
GMLP_DIM = GMLP_GROUPS * GMLP_GROUP_DIM
GMLP_CHUNK = 128
SSM_DIM = 1024
SSM_HEAD_DIM = 64
SSM_HEADS = SSM_DIM // SSM_HEAD_DIM
SSM_GROUPS = 2
SSM_STATE = 128
SSM_CONV = 4
SSM_XBC = SSM_DIM + 2 * SSM_GROUPS * SSM_STATE

M_P = BATCH * SEQ
M_S = DEC_BATCH * DEC_SEQ
M_ALL = M_P + M_S

OFF_F = 3 * CONV_DIM + 3 * FOX_DIM
OFF_U = OFF_F + FOX_HEADS
OFF_DT = OFF_U + 2 * GMLP_DIM + SSM_DIM + SSM_XBC
OFF_G = OFF_DT + SSM_HEADS
N_ZC = 2 * GMLP_DIM + SSM_DIM + SSM_XBC

V7X_VMEM_LIMIT_BYTES = 56 * 1024 * 1024
NEG_BIG = -1e30
ATT_SCALE = 1.0 / math.sqrt(FOX_HEAD_DIM)
SSM_Q_PROMPT = 256
GMLP_STEP_CHUNKS = 4


def _cp(*sem):
    return pltpu.CompilerParams(dimension_semantics=sem, vmem_limit_bytes=V7X_VMEM_LIMIT_BYTES)


def _silu(x):
    return x * jax.nn.sigmoid(x)


def _rmsnorm_body(x_ref, g_ref, o_ref):
    x = x_ref[...]
    ms = jnp.mean(x * x, axis=-1, keepdims=True)
    o_ref[...] = (x * lax.rsqrt(ms + EPS) * g_ref[...]).astype(o_ref.dtype)


def rmsnorm_bf16(x, g, tm=384):
    m, d = x.shape
    return pl.pallas_call(
        _rmsnorm_body,
        grid=(m // tm,),
        in_specs=[pl.BlockSpec((tm, d), lambda i: (i, 0)),
                  pl.BlockSpec((1, d), lambda i: (0, 0))],
        out_specs=pl.BlockSpec((tm, d), lambda i: (i, 0)),
        out_shape=jax.ShapeDtypeStruct((m, d), BF16),
        compiler_params=_cp("parallel"),
        name="rmsnorm",
    )(x, g.reshape(1, d))


def _mm_body(a_ref, w_ref, *refs, nk, scale, has_res):
    if has_res:
        x_ref, o_ref, *scr = refs
    else:
        x_ref = None
        o_ref, *scr = refs
    d = jnp.dot(a_ref[...], w_ref[...], preferred_element_type=F32)

    def finish(acc):
        if has_res:
            acc = x_ref[...] + scale * acc
        o_ref[...] = acc.astype(o_ref.dtype)

    if nk == 1:
        finish(d)
    else:
        acc_ref = scr[0]
        k = pl.program_id(2)

        @pl.when(k == 0)
        def _():
            acc_ref[...] = d

        @pl.when(jnp.logical_and(k > 0, k < nk - 1))
        def _():
            acc_ref[...] += d

        @pl.when(k == nk - 1)
        def _():
            finish(acc_ref[...] + d)


def matmul(a, w, *, tm, tn, nk=1, layer=None, res=None, scale=1.0, out_dtype=F32, name="matmul"):
    m, kdim = a.shape
    n = w.shape[-1]
    tk = kdim // nk
    if layer is None:
        w_spec = pl.BlockSpec((tk, tn), lambda i, j, k: (k, j))
    else:
        w_spec = pl.BlockSpec((None, tk, tn), lambda i, j, k: (layer, k, j))
    in_specs = [pl.BlockSpec((tm, tk), lambda i, j, k: (i, k)), w_spec]
    args = [a, w]
    if res is not None:
        in_specs.append(pl.BlockSpec((tm, tn), lambda i, j, k: (i, j)))
        args.append(res)
    scratch = [pltpu.VMEM((tm, tn), F32)] if nk > 1 else []
    return pl.pallas_call(
        functools.partial(_mm_body, nk=nk, scale=scale, has_res=res is not None),
        grid=(m // tm, n // tn, nk),
        in_specs=in_specs,
        out_specs=pl.BlockSpec((tm, tn), lambda i, j, k: (i, j)),
        out_shape=jax.ShapeDtypeStruct((m, n), out_dtype),
        scratch_shapes=scratch,
        compiler_params=_cp("parallel", "parallel", "arbitrary"),
        name=name,
    )(*args)


LANES = 128
SUBLANES = 8
NT_DIMS = (((1,), (1,)), ((), ()))


def _wt_rows_spec(tn, kdim, layer, row0, step_of):
    if row0 % tn == 0:
        return pl.BlockSpec((None, tn, kdim), lambda *g: (layer, row0 // tn + step_of(*g), 0))
    assert row0 % SUBLANES == 0 and tn % SUBLANES == 0

    def index_map(*g):
        row = row0 + tn * step_of(*g)
        if not isinstance(row, int):
            row = pl.multiple_of(row, SUBLANES)
        return layer, row, 0

    return pl.BlockSpec((None, pl.Element(tn), pl.Element(kdim)), index_map)


def _mm_wst_body(a_ref, w_ref, o_ref, wb_ref):
    @pl.when(pl.program_id(1) == 0)
    def _():
        wb_ref[...] = w_ref[...].astype(BF16)

    o_ref[...] = lax.dot_general(a_ref[...], wb_ref[...], NT_DIMS,
                                 preferred_element_type=F32).astype(o_ref.dtype)


def matmul_wst(a, wt_stack, layer, *, row0, n, tm, tn, out_dtype=F32, name="matmul_wst"):
    m, kdim = a.shape
    return pl.pallas_call(
        _mm_wst_body,
        grid=(n // tn, m // tm),
        in_specs=[pl.BlockSpec((tm, kdim), lambda j, i: (i, 0)),
                  _wt_rows_spec(tn, kdim, layer, row0, lambda j, i: j)],
        out_specs=pl.BlockSpec((tm, tn), lambda j, i: (i, j)),
        out_shape=jax.ShapeDtypeStruct((m, n), out_dtype),
        scratch_shapes=[pltpu.VMEM((tn, kdim), BF16)],
        compiler_params=_cp("arbitrary", "arbitrary"),
        name=name,
    )(a, wt_stack)


def _cast_rows_body(w_ref, o_ref):
    o_ref[...] = w_ref[...].astype(o_ref.dtype)


def cast_rows(wt_stack, layer, *, row0, n, tr=512):
    kdim = wt_stack.shape[2]
    return pl.pallas_call(
        _cast_rows_body,
        grid=(n // tr,),
        in_specs=[_wt_rows_spec(tr, kdim, layer, row0, lambda r: r)],
        out_specs=pl.BlockSpec((tr, kdim), lambda r: (r, 0)),
        out_shape=jax.ShapeDtypeStruct((n, kdim), BF16),
        compiler_params=_cp("parallel"),
        name="cast_rows",
    )(wt_stack)


def _in_proj_s_body(h_ref, wf_ref, wdt_ref, o_ref):
    pad = jnp.zeros((LANES - FOX_HEADS - SSM_HEADS, wf_ref.shape[1]), F32)
    w = jnp.concatenate([wf_ref[...], wdt_ref[...], pad], axis=0).astype(BF16)
    o_ref[...] = lax.dot_general(h_ref[...], w, NT_DIMS, preferred_element_type=F32)


def in_proj_small(h, wt_stack, layer, tm=1056):
    m, kdim = h.shape
    return pl.pallas_call(
        _in_proj_s_body,
        grid=(m // tm,),
        in_specs=[pl.BlockSpec((tm, kdim), lambda i: (i, 0)),
                  _wt_rows_spec(FOX_HEADS, kdim, layer, OFF_F, lambda i: 0),
                  _wt_rows_spec(SSM_HEADS, kdim, layer, OFF_DT, lambda i: 0)],
        out_specs=pl.BlockSpec((tm, LANES), lambda i: (i, 0)),
        out_shape=jax.ShapeDtypeStruct((m, LANES), F32),
        compiler_params=_cp("parallel"),
        name="in_proj_s",
    )(h, wt_stack, wt_stack)


def _mm_ws_body(a_ref, w_ref, *refs, scale, has_res):
    if has_res:
        x_ref, o_ref, wb_ref = refs
    else:
        x_ref = None
        o_ref, wb_ref = refs

    @pl.when(pl.program_id(1) == 0)
    def _():
        wb_ref[...] = w_ref[...].astype(BF16)

    d = jnp.dot(a_ref[...], wb_ref[...], preferred_element_type=F32)
    if has_res:
        d = x_ref[...] + scale * d
    o_ref[...] = d.astype(o_ref.dtype)


def matmul_ws(a, w_stack, layer, *, n, tm, tn, col0=0, res=None, scale=1.0, out_dtype=F32, name="matmul_ws"):
    m, kdim = a.shape
    cb0 = col0 // tn
    in_specs = [pl.BlockSpec((tm, kdim), lambda j, i: (i, 0)),
                pl.BlockSpec((None, kdim, tn), lambda j, i: (layer, 0, cb0 + j))]
    args = [a, w_stack]
    if res is not None:
        in_specs.append(pl.BlockSpec((tm, tn), lambda j, i: (i, j)))
        args.append(res)
    return pl.pallas_call(
        functools.partial(_mm_ws_body, scale=scale, has_res=res is not None),
        grid=(n // tn, m // tm),
        in_specs=in_specs,
        out_specs=pl.BlockSpec((tm, tn), lambda j, i: (i, j)),
        out_shape=jax.ShapeDtypeStruct((m, n), out_dtype),
        scratch_shapes=[pltpu.VMEM((kdim, tn), BF16)],
        compiler_params=_cp("arbitrary", "arbitrary"),
        name=name,
    )(*args)


def _ffn_up_body(h_ref, wg_ref, wu_ref, o_ref, wgb_ref, wub_ref):
    @pl.when(pl.program_id(1) == 0)
    def _():
        wgb_ref[...] = wg_ref[...].astype(BF16)
        wub_ref[...] = wu_ref[...].astype(BF16)

    h = h_ref[...]
    g = jnp.dot(h, wgb_ref[...], preferred_element_type=F32)
    u = jnp.dot(h, wub_ref[...], preferred_element_type=F32)
    o_ref[...] = (_silu(g) * u).astype(o_ref.dtype)


def ffn_up(h, wg_stack, wu_stack, layer, tm=1408, tn=256):
    m, d = h.shape
    n = wg_stack.shape[2]
    w_spec = pl.BlockSpec((None, d, tn), lambda j, i: (layer, 0, j))
    return pl.pallas_call(
        _ffn_up_body,
        grid=(n // tn, m // tm),
        in_specs=[pl.BlockSpec((tm, d), lambda j, i: (i, 0)), w_spec, w_spec],
        out_specs=pl.BlockSpec((tm, tn), lambda j, i: (i, j)),
        out_shape=jax.ShapeDtypeStruct((m, n), BF16),
        scratch_shapes=[pltpu.VMEM((d, tn), BF16), pltpu.VMEM((d, tn), BF16)],
        compiler_params=_cp("arbitrary", "arbitrary"),
        name="ffn_up",
    )(h, wg_stack, wu_stack)


def ffn(x, norm_g, wg_stack, wu_stack, wd_stack_bf16, layer):
    h = rmsnorm_bf16(x, norm_g)
    g = ffn_up(h, wg_stack, wu_stack, layer)
    return matmul(g, wd_stack_bf16, layer=layer, tm=1056, tn=512, nk=2, res=x, scale=0.5, name="ffn_down")


def _merge_body(h_ref, ya_ref, yb_ref, yc_ref, yd_ref, g0_ref, g1_ref, g2_ref, g3_ref,
                wa_ref, wb_ref, wc_ref, wd_ref, bias_ref, o_ref):
    h = h_ref[...]
    acc = None
    branches = ((ya_ref, g0_ref, wa_ref), (yb_ref, g1_ref, wb_ref),
                (yc_ref, g2_ref, wc_ref), (yd_ref, g3_ref, wd_ref))
    for b, (y_ref, g_ref, w_ref) in enumerate(branches):
        gate = jax.nn.sigmoid(lax.dot_general(h, g_ref[...], NT_DIMS, preferred_element_type=F32)
                              + bias_ref[b:b + 1, :])
        t = gate * jnp.dot(y_ref[...], w_ref[...], preferred_element_type=F32)
        acc = t if acc is None else acc + t
    o_ref[...] = acc.astype(o_ref.dtype)


def merge(h, ys, wt_gate, w_ups, layer, gate_bias, tm=704, tn=256):
    m, d = h.shape
    nb = D_MODEL // tn
    y_spec = pl.BlockSpec((tm, CONV_DIM), lambda i, j: (i, 0))
    in_specs = [pl.BlockSpec((tm, d), lambda i, j: (i, 0))] + [y_spec] * 4
    for b in range(N_BRANCH):
        in_specs.append(pl.BlockSpec((tn, d), lambda i, j, b=b: (b * nb + j, 0)))
    in_specs += [pl.BlockSpec((None, CONV_DIM, tn), lambda i, j: (layer, 0, j))] * 4
    in_specs.append(pl.BlockSpec((N_BRANCH, tn), lambda i, j: (0, j)))
    return pl.pallas_call(
        _merge_body,
        grid=(m // tm, nb),
        in_specs=in_specs,
        out_specs=pl.BlockSpec((tm, tn), lambda i, j: (i, j)),
        out_shape=jax.ShapeDtypeStruct((m, D_MODEL), BF16),
        compiler_params=_cp("parallel", "parallel"),
        name="merge",
    )(h, *ys, wt_gate, wt_gate, wt_gate, wt_gate, *w_ups, gate_bias)


def _alias_rows(ybuf, in_specs, args):
    if ybuf is None:
        return {}
    in_specs.append(pl.BlockSpec(memory_space=pl.ANY))
    args.append(ybuf)
    return {len(args) - 1: 0}


def _conv_a_body(xa_ref, gb_ref, gc_ref, buf_ref, w_ref, *refs, L):
    y_ref, nb_ref, scr_ref = refs[-3:]
    t = gc_ref[...] * xa_ref[...]
    scr_ref[pl.ds(8, L), :] = t
    scr_ref[pl.ds(6, 2), :] = buf_ref[0]
    w = w_ref[...]
    y = scr_ref[pl.ds(6, L), :] * w[0:1] + scr_ref[pl.ds(7, L), :] * w[1:2] + t * w[2:3]
    y_ref[...] = (gb_ref[...] * y).astype(y_ref.dtype)
    nb_ref[0] = scr_ref[pl.ds(6 + L, 2), :]


def conv_a(za, buf, w, *, L, nseq, row0, ybuf=None, cb=256):
    ncb = CONV_DIM // cb
    rb = row0 // L

    def zspec(grp):
        return pl.BlockSpec((L, cb), lambda s, c: (rb + s, grp * ncb + c))

    in_specs = [zspec(0), zspec(1), zspec(2),
                pl.BlockSpec((1, CONV_WIDTH - 1, cb), lambda s, c: (s, 0, c)),
                pl.BlockSpec((CONV_WIDTH, cb), lambda s, c: (0, c))]
    args = [za, za, za, buf, w]
    aliases = _alias_rows(ybuf, in_specs, args)
    return pl.pallas_call(
        functools.partial(_conv_a_body, L=L),
        grid=(nseq, ncb),
        in_specs=in_specs,
        out_specs=[pl.BlockSpec((L, cb), lambda s, c: (rb + s, c)),
                   pl.BlockSpec((1, CONV_WIDTH - 1, cb), lambda s, c: (s, 0, c))],
        out_shape=[jax.ShapeDtypeStruct((M_ALL, CONV_DIM), BF16),
                   jax.ShapeDtypeStruct((nseq, CONV_WIDTH - 1, CONV_DIM), F32)],
        scratch_shapes=[pltpu.VMEM((L + 8, cb), F32)],
        input_output_aliases=aliases,
        compiler_params=_cp("parallel", "parallel"),
        name="conv_a",
    )(*args)


def _qknorm_body(q_ref, k_ref, v_ref, gq_ref, gk_ref, *refs):
    qn_ref, knb_ref, vb_ref, kn_ref, vf_ref = refs[-5:]
    gq = gq_ref[...]
    gk = gk_ref[...]
    for h in range(FOX_HEADS):
        sl = slice(h * FOX_HEAD_DIM, (h + 1) * FOX_HEAD_DIM)
        q = q_ref[:, sl]
        qn = q * lax.rsqrt(jnp.mean(q * q, axis=-1, keepdims=True) + EPS) * gq
        qn_ref[:, sl] = qn.astype(qn_ref.dtype)
        k = k_ref[:, sl]
        kn = k * lax.rsqrt(jnp.mean(k * k, axis=-1, keepdims=True) + EPS) * gk
        kn_ref[:, sl] = kn
        knb_ref[:, sl] = kn.astype(knb_ref.dtype)
    v = v_ref[...]
    vf_ref[...] = v
    vb_ref[...] = v.astype(vb_ref.dtype)


def qk_norm(za, gq, gk, *, rows, row0, tm, layer=None, kn_stack=None, v_stack=None):
    rb = row0 // tm

    def zspec(grp):
        return pl.BlockSpec((tm, FOX_DIM), lambda i: (rb + i, grp))

    ospec = pl.BlockSpec((tm, FOX_DIM), lambda i: (i, 0))
    gspec = pl.BlockSpec((1, FOX_HEAD_DIM), lambda i: (0, 0))
    in_specs = [zspec(3), zspec(4), zspec(5), gspec, gspec]
    args = [za, za, za, gq.reshape(1, -1), gk.reshape(1, -1)]
    aliases = {}
    if layer is None:
        f32_spec = ospec
        f32_shape = jax.ShapeDtypeStruct((rows, FOX_DIM), F32)
    else:
        f32_spec = pl.BlockSpec((None, tm, FOX_DIM), lambda i: (layer, i, 0))
        f32_shape = jax.ShapeDtypeStruct((DEPTH, rows, FOX_DIM), F32)
        if kn_stack is not None:
            in_specs += [pl.BlockSpec(memory_space=pl.ANY)] * 2
            args += [kn_stack, v_stack]
            aliases = {5: 3, 6: 4}
    bf_shape = jax.ShapeDtypeStruct((rows, FOX_DIM), BF16)
    return pl.pallas_call(
        _qknorm_body,
        grid=(rows // tm,),
        in_specs=in_specs,
        out_specs=[ospec, ospec, ospec, f32_spec, f32_spec],
        out_shape=[bf_shape, bf_shape, bf_shape, f32_shape, f32_shape],
        input_output_aliases=aliases,
        compiler_params=_cp("parallel"),
        name="qk_norm",
    )(*args)


def _flash_body(q_ref, k_ref, v_ref, fq_ref, fkt_ref, o_ref, m_ref, l_ref, acc_ref, *, tq, tk, nk):
    qi = pl.program_id(1)
    ki = pl.program_id(2)
    lanes = FOX_HEAD_DIM

    @pl.when(ki == 0)
    def _():
        m_ref[...] = jnp.full(m_ref.shape, NEG_BIG, F32)
        l_ref[...] = jnp.zeros(l_ref.shape, F32)
        acc_ref[...] = jnp.zeros(acc_ref.shape, F32)

    def step(on_diagonal):
        fq = fq_ref[...]
        fkt = fkt_ref[...]
        if on_diagonal:
            mask = (lax.broadcasted_iota(jnp.int32, (tq, tk), 1)
                    <= lax.broadcasted_iota(jnp.int32, (tq, tk), 0))
        for h in range(FOX_HEADS):
            sl = slice(h * FOX_HEAD_DIM, (h + 1) * FOX_HEAD_DIM)
            s = lax.dot_general(q_ref[:, sl], k_ref[:, sl], (((1,), (1,)), ((), ())),
                                preferred_element_type=F32) * ATT_SCALE
            s = s + (fq[:, h:h + 1] - fkt[h:h + 1, :])
            if on_diagonal:
                s = jnp.where(mask, s, NEG_BIG)
            m_prev = m_ref[h]
            m_new = jnp.maximum(m_prev, jnp.max(s, axis=-1, keepdims=True))
            alpha = jnp.exp(m_prev - m_new)
            p = jnp.exp(s - jnp.concatenate([m_new] * (tk // lanes), axis=1))
            l_ref[h] = alpha * l_ref[h] + jnp.sum(p, axis=-1, keepdims=True)
            acc_ref[:, sl] = alpha * acc_ref[:, sl] + jnp.dot(
                p.astype(BF16), v_ref[:, sl], preferred_element_type=F32)
            m_ref[h] = m_new

    @pl.when(ki < qi)
    def _():
        step(False)

    @pl.when(ki == qi)
    def _():
        step(True)

    @pl.when(ki == nk - 1)
    def _():
        for h in range(FOX_HEADS):
            sl = slice(h * FOX_HEAD_DIM, (h + 1) * FOX_HEAD_DIM)
            o_ref[:, sl] = (acc_ref[:, sl] / l_ref[h]).astype(o_ref.dtype)


def fox_prompt(qn, knb, vb, f_cum, f_cum_t, t=512):
    nb = SEQ // t
    kv_spec = pl.BlockSpec((t, FOX_DIM), lambda b, qi, ki: (b * nb + jnp.minimum(ki, qi), 0))
    return pl.pallas_call(
        functools.partial(_flash_body, tq=t, tk=t, nk=nb),
        grid=(BATCH, nb, nb),
        in_specs=[pl.BlockSpec((t, FOX_DIM), lambda b, qi, ki: (b * nb + qi, 0)),
                  kv_spec, kv_spec,
                  pl.BlockSpec((t, FOX_HEADS), lambda b, qi, ki: (b * nb + qi, 0)),
                  pl.BlockSpec((FOX_HEADS, t), lambda b, qi, ki: (0, b * nb + jnp.minimum(ki, qi)))],
        out_specs=pl.BlockSpec((t, FOX_DIM), lambda b, qi, ki: (b * nb + qi, 0)),
        out_shape=jax.ShapeDtypeStruct((M_ALL, FOX_DIM), BF16),
        scratch_shapes=[pltpu.VMEM((FOX_HEADS, t, FOX_HEAD_DIM), F32),
                        pltpu.VMEM((FOX_HEADS, t, FOX_HEAD_DIM), F32),
                        pltpu.VMEM((t, FOX_DIM), F32)],
        compiler_params=_cp("parallel", "parallel", "arbitrary"),
        name="fox_prompt",
    )(qn, knb, vb, f_cum, f_cum_t)


def _fox_sample_body(q_ref, kc_ref, vc_ref, kn_ref, vn_ref, bc_ref, bn_ref, *refs):
    o_ref = refs[-1]
    rows = FOX_HEADS * DEC_SEQ
    q = q_ref[...]
    qrep = jnp.concatenate([q] * FOX_HEADS, axis=0)
    r_head = lax.broadcasted_iota(jnp.int32, (rows, FOX_DIM), 0) // DEC_SEQ
    c_head = lax.broadcasted_iota(jnp.int32, (rows, FOX_DIM), 1) // FOX_HEAD_DIM
    own = r_head == c_head
    qbd = jnp.where(own, qrep, jnp.zeros_like(qrep))
    nt = (((1,), (1,)), ((), ()))
    pad = jnp.zeros((rows - DEC_SEQ, FOX_DIM), BF16)
    kn = jnp.concatenate([kn_ref[...], pad], axis=0)
    vn = jnp.concatenate([vn_ref[...], pad], axis=0)

    def heads_on_lanes(c_ref):
        return jnp.concatenate(
            [c_ref[0, pl.ds(h, PAST_LEN, stride=FOX_HEADS), :].astype(BF16) for h in range(FOX_HEADS)], axis=1)

    kc = heads_on_lanes(kc_ref)
    vc = heads_on_lanes(vc_ref)
    s_c = lax.dot_general(qbd, kc, nt, preferred_element_type=F32) * ATT_SCALE + bc_ref[0]
    s_n = lax.dot_general(qbd, kn, nt, preferred_element_type=F32) * ATT_SCALE + bn_ref[0]
    m = jnp.maximum(jnp.max(s_c, axis=-1, keepdims=True), jnp.max(s_n, axis=-1, keepdims=True))
    p_c = jnp.exp(s_c - m)
    p_n = jnp.exp(s_n - m)
    inv = 1.0 / (jnp.sum(p_c, axis=-1, keepdims=True) + jnp.sum(p_n, axis=-1, keepdims=True))
    o_full = (jnp.dot((p_c * inv).astype(BF16), vc, preferred_element_type=F32)
              + jnp.dot((p_n * inv).astype(BF16), vn, preferred_element_type=F32))
    o_full = jnp.where(own, o_full, 0.0)
    o = o_full[0:DEC_SEQ]
    for h in range(1, FOX_HEADS):
        o = o + o_full[h * DEC_SEQ:(h + 1) * DEC_SEQ]
    o_ref[...] = o.astype(o_ref.dtype)


def fox_sample(qn_s, knb_s, vb_s, cache_k, cache_v, layer, bias_c, bias_n, ybuf):
    rows = FOX_HEADS * DEC_SEQ
    new_spec = pl.BlockSpec((DEC_SEQ, FOX_DIM), lambda b: (b, 0))
    cache_spec = pl.BlockSpec((None, 1, PAST_LEN * FOX_HEADS, FOX_HEAD_DIM), lambda b: (layer, b, 0, 0))
    in_specs = [new_spec, cache_spec, cache_spec, new_spec, new_spec,
                pl.BlockSpec((1, rows, PAST_LEN), lambda b: (b, 0, 0)),
                pl.BlockSpec((1, rows, rows), lambda b: (b, 0, 0))]
    args = [qn_s, cache_k, cache_v, knb_s, vb_s, bias_c, bias_n]
    aliases = _alias_rows(ybuf, in_specs, args)
    return pl.pallas_call(
        _fox_sample_body,
        grid=(DEC_BATCH,),
        in_specs=in_specs,
        out_specs=pl.BlockSpec((DEC_SEQ, FOX_DIM), lambda b: (M_P // DEC_SEQ + b, 0)),
        out_shape=jax.ShapeDtypeStruct((M_ALL, FOX_DIM), BF16),
        input_output_aliases=aliases,
        compiler_params=_cp("parallel"),
        name="fox_sample",
    )(*args)


def _gmlp_body(u_ref, vg_ref, gn_ref, w_ref, b_ref, *refs, emit_vn):
    if emit_vn:
        y_ref, vn_ref = refs[-2:]
    else:
        y_ref, vn_ref = refs[-1], None
    v = jax.nn.gelu(vg_ref[...])
    vn = v * lax.rsqrt(jnp.mean(v * v, axis=-1, keepdims=True) + EPS) * gn_ref[...]
    if emit_vn:
        vn_ref[...] = vn
    vb = vn.astype(BF16)
    bias = b_ref[...]
    for g in range(GMLP_GROUPS):
        sl = slice(g * GMLP_GROUP_DIM, (g + 1) * GMLP_GROUP_DIM)
        s = jnp.dot(w_ref[g], vb[:, sl], preferred_element_type=F32) + bias[:, g:g + 1]
        y_ref[:, sl] = (jax.nn.gelu(u_ref[:, sl]) * s).astype(y_ref.dtype)


def gmlp(zc, gn, w_eff, b_eff, *, R, nchunk, row0, emit_vn, ybuf=None):
    rb = row0 // R
    out_specs = [pl.BlockSpec((R, GMLP_DIM), lambda c: (rb + c, 0))]
    out_shape = [jax.ShapeDtypeStruct((M_ALL, GMLP_DIM), BF16)]
    if emit_vn:
        out_specs.append(pl.BlockSpec((R, GMLP_DIM), lambda c: (c, 0)))
        out_shape.append(jax.ShapeDtypeStruct((nchunk * R, GMLP_DIM), F32))
    in_specs = [pl.BlockSpec((R, GMLP_DIM), lambda c: (rb + c, 0)),
                pl.BlockSpec((R, GMLP_DIM), lambda c: (rb + c, 1)),
                pl.BlockSpec((1, GMLP_DIM), lambda c: (0, 0)),
                pl.BlockSpec((GMLP_GROUPS, R, R), lambda c: (0, 0, 0)),
                pl.BlockSpec((R, GMLP_GROUPS), lambda c: (0, 0))]
    args = [zc, zc, gn.reshape(1, -1), w_eff, b_eff]
    aliases = _alias_rows(ybuf, in_specs, args)
    return pl.pallas_call(
        functools.partial(_gmlp_body, emit_vn=emit_vn),
        grid=(nchunk,),
        in_specs=in_specs,
        out_specs=out_specs,
        out_shape=out_shape,
        input_output_aliases=aliases,
        compiler_params=_cp("parallel"),
        name="gmlp",
    )(*args)


def _ssm_body(z_ref, xbc_ref, dt_ref, adtt_ref, a_ref, cbuf_ref, h0_ref, cw_ref, cb_ref, dfull_ref, nw_ref,
              *refs, Q):
    y_ref, ncb_ref, hs_out_ref, scr_ref, hs_ref = refs[-5:]
    c = pl.program_id(1)
    nc = pl.num_programs(1)
    tail = SSM_CONV - 1
    lo = 8 - tail

    @pl.when(c == 0)
    def _():
        scr_ref[pl.ds(lo, tail), :] = cbuf_ref[0]
        hs_ref[...] = h0_ref[0]

    xbc = xbc_ref[...]
    scr_ref[pl.ds(8, Q), :] = xbc
    cw = cw_ref[...]
    conv = xbc * cw[tail:tail + 1] + cb_ref[...]
    for i in range(tail):
        conv = conv + scr_ref[pl.ds(lo + i, Q), :] * cw[i:i + 1]
    xc = _silu(conv)
    new_tail = scr_ref[pl.ds(lo + Q, tail), :]
    ncb_ref[0] = new_tail
    scr_ref[pl.ds(lo, tail), :] = new_tail

    xs = xc[:, :SSM_DIM]
    gw = SSM_STATE
    dt = dt_ref[...]
    adt = dt * a_ref[...]
    ii = lax.broadcasted_iota(jnp.int32, (Q, Q), 0)
    jj = lax.broadcasted_iota(jnp.int32, (Q, Q), 1)
    tri = ii >= jj
    acum = jnp.dot(tri.astype(F32), adt, precision=lax.Precision.HIGHEST,
                   preferred_element_type=F32)
    acum_t = jnp.dot(adtt_ref[0], (ii <= jj).astype(F32), precision=lax.Precision.HIGHEST,
                     preferred_element_type=F32)
    a_last = acum[Q - 1:Q, :]
    lane_lo = lax.broadcasted_iota(jnp.int32, (Q, 2 * SSM_HEAD_DIM), 1) < SSM_HEAD_DIM
    row_lo = lax.broadcasted_iota(jnp.int32, (2 * SSM_HEAD_DIM, SSM_STATE), 0) < SSM_HEAD_DIM
    nt = (((1,), (1,)), ((), ()))
    tn = (((0,), (0,)), ((), ()))
    dfull = dfull_ref[...]
    ys = []
    s_g = None
    for pr in range(SSM_HEADS // 2):
        g = (2 * pr) // (SSM_HEADS // SSM_GROUPS)
        b_g = xc[:, SSM_DIM + g * gw:SSM_DIM + (g + 1) * gw]
        c_g = xc[:, SSM_DIM + SSM_GROUPS * gw + g * gw:SSM_DIM + SSM_GROUPS * gw + (g + 1) * gw]
        c_gb = c_g.astype(BF16)
        if (2 * pr) % (SSM_HEADS // SSM_GROUPS) == 0:
            s_g = lax.dot_general(c_gb, b_g.astype(BF16), nt, preferred_element_type=F32)
        h0i, h1i = 2 * pr, 2 * pr + 1
        sl = slice(pr * 2 * SSM_HEAD_DIM, (pr + 1) * 2 * SSM_HEAD_DIM)
        xs_p = xs[:, sl]
        a0 = acum[:, h0i:h0i + 1]
        a1 = acum[:, h1i:h1i + 1]
        l0 = jnp.exp(jnp.where(tri, a0 - acum_t[h0i:h0i + 1, :], -jnp.inf))
        l1 = jnp.exp(jnp.where(tri, a1 - acum_t[h1i:h1i + 1, :], -jnp.inf))
        xdt = (xs_p * jnp.where(lane_lo, dt[:, h0i:h0i + 1], dt[:, h1i:h1i + 1])).astype(BF16)
        y = jnp.where(lane_lo,
                      jnp.dot((s_g * l0).astype(BF16), xdt, preferred_element_type=F32),
                      jnp.dot((s_g * l1).astype(BF16), xdt, preferred_element_type=F32))
        hs_p = hs_ref[pr]
        y_off = lax.dot_general(c_gb, hs_p.astype(BF16), nt, preferred_element_type=F32)
        y = y + y_off * jnp.exp(jnp.where(lane_lo, a0, a1)) + dfull[:, sl] * xs_p
        ys.append(y)
        al0 = a_last[:, h0i:h0i + 1]
        al1 = a_last[:, h1i:h1i + 1]
        bw0 = (b_g * jnp.exp(al0 - a0)).astype(BF16)
        bw1 = (b_g * jnp.exp(al1 - a1)).astype(BF16)
        r0 = lax.dot_general(xdt, bw0, tn, preferred_element_type=F32)
        r1 = lax.dot_general(xdt, bw1, tn, preferred_element_type=F32)
        decay = jnp.where(row_lo, jnp.exp(al0), jnp.exp(al1))
        hs_ref[pr] = hs_p * decay + jnp.where(row_lo, r0, r1)

    z = z_ref[...]
    y_all = jnp.concatenate(ys, axis=1) * _silu(z)
    gsz = SSM_DIM // SSM_GROUPS
    nw = nw_ref[...]
    for g in range(SSM_GROUPS):
        sl = slice(g * gsz, (g + 1) * gsz)
        seg = y_all[:, sl]
        y_ref[:, sl] = (seg * lax.rsqrt(jnp.mean(seg * seg, axis=-1, keepdims=True) + EPS)
                        * nw[:, sl]).astype(y_ref.dtype)

    @pl.when(c == nc - 1)
    def _():
        hs_out_ref[0] = hs_ref[...]


def ssm(zc, dt, adt_t, adt_t_index, a_row, cbuf, h0, cw, cb, dfull, nw, *, Q, nseq, L, row0, ybuf=None):
    nchunk = L // Q
    rb = row0 // Q
    npair = SSM_HEADS // 2
    in_specs = [pl.BlockSpec((Q, SSM_DIM), lambda s, c: (rb + s * nchunk + c, 2)),
                pl.BlockSpec((Q, SSM_XBC), lambda s, c: (rb + s * nchunk + c, 2)),
                pl.BlockSpec((Q, SSM_HEADS), lambda s, c: (s * nchunk + c, 0)),
                pl.BlockSpec((1, SSM_HEADS, Q), adt_t_index),
                pl.BlockSpec((1, SSM_HEADS), lambda s, c: (0, 0)),
                pl.BlockSpec((1, SSM_CONV - 1, SSM_XBC), lambda s, c: (s, 0, 0)),
                pl.BlockSpec((1, npair, 2 * SSM_HEAD_DIM, SSM_STATE), lambda s, c: (s, 0, 0, 0)),
                pl.BlockSpec((SSM_CONV, SSM_XBC), lambda s, c: (0, 0)),
                pl.BlockSpec((1, SSM_XBC), lambda s, c: (0, 0)),
                pl.BlockSpec((1, SSM_DIM), lambda s, c: (0, 0)),
                pl.BlockSpec((1, SSM_DIM), lambda s, c: (0, 0))]
    args = [zc, zc, dt, adt_t, a_row, cbuf, h0, cw, cb, dfull, nw]
    aliases = _alias_rows(ybuf, in_specs, args)
    return pl.pallas_call(
        functools.partial(_ssm_body, Q=Q),
        grid=(nseq, nchunk),
        in_specs=in_specs,
        out_specs=[pl.BlockSpec((Q, SSM_DIM), lambda s, c: (rb + s * nchunk + c, 0)),
                   pl.BlockSpec((1, SSM_CONV - 1, SSM_XBC), lambda s, c: (s, 0, 0)),
                   pl.BlockSpec((1, npair, 2 * SSM_HEAD_DIM, SSM_STATE), lambda s, c: (s, 0, 0, 0))],
        out_shape=[jax.ShapeDtypeStruct((M_ALL, SSM_DIM), BF16),
                   jax.ShapeDtypeStruct((nseq, SSM_CONV - 1, SSM_XBC), F32),
                   jax.ShapeDtypeStruct((nseq, npair, 2 * SSM_HEAD_DIM, SSM_STATE), F32)],
        scratch_shapes=[pltpu.VMEM((Q + 8, SSM_XBC), F32),
                        pltpu.VMEM((npair, 2 * SSM_HEAD_DIM, SSM_STATE), F32)],
        input_output_aliases=aliases,
        compiler_params=_cp("parallel", "arbitrary"),
        name="ssm",
    )(*args)


def _cumsum_lanes(x):
    nb = x.shape[-1] // LANES
    xb = x.reshape(x.shape[:-1] + (nb, LANES))
    tri = (jnp.arange(LANES)[:, None] <= jnp.arange(LANES)[None, :]).astype(F32)
    local = jnp.einsum('...bk,kj->...bj', xb, tri, precision=lax.Precision.HIGHEST)
    total = local[..., -1]
    carry = jnp.cumsum(total, axis=-1) - total
    return (local + carry[..., None]).reshape(x.shape)


def _token_mix(x, l, p, st, kn_stack, v_stack):
    h = rmsnorm_bf16(x, p['mix_norm'][l])
    wt_in = p['wt_in']
    za = matmul_wst(h, wt_in, l, row0=0, n=OFF_F, tm=1408, tn=512, name="in_proj_a")
    zc = matmul_wst(h, wt_in, l, row0=OFF_U, n=N_ZC, tm=1408, tn=512, name="in_proj_c")
    zs = in_proj_small(h, wt_in, l)
    zs_t = zs.T
    logf_t = jax.nn.log_sigmoid(zs_t[:FOX_HEADS] + p['fox_f_bias'][l][:, None])
    dt_t = jax.nn.softplus(zs_t[FOX_HEADS:FOX_HEADS + SSM_HEADS] + p['ssm_dt_bias'][l][:, None])

    conv_w = p['conv_a_w'][l]
    ya, nca_p = conv_a(za, jnp.zeros((BATCH, CONV_WIDTH - 1, CONV_DIM), F32), conv_w,
                       L=SEQ, nseq=BATCH, row0=0)
    ya, nca_s = conv_a(za, st['conv_a'][l], conv_w, L=DEC_SEQ, nseq=DEC_BATCH, row0=M_P, ybuf=ya)

    gq, gk = p['fox_q_norm'][l], p['fox_k_norm'][l]
    qn_p, knb_p, vb_p, kn_stack, v_stack = qk_norm(za, gq, gk, rows=M_P, row0=0, tm=512, layer=l,
                                                   kn_stack=kn_stack, v_stack=v_stack)
    qn_s, knb_s, vb_s, kn_s, v_s = qk_norm(za, gq, gk, rows=M_S, row0=M_P, tm=M_S)
    f_t_p = _cumsum_lanes(logf_t[:, :M_P].reshape(FOX_HEADS, BATCH, SEQ)).reshape(FOX_HEADS, M_P)
    yb = fox_prompt(qn_p, knb_p, vb_p, f_t_p.T, f_t_p)
    logf_s_t = jnp.transpose(logf_t[:, M_P:].reshape(FOX_HEADS, DEC_BATCH, DEC_SEQ), (1, 0, 2))
    f_cache_t = _cumsum_lanes(jnp.transpose(st['fox_logf'][l], (0, 2, 1)))
    f_new_t = f_cache_t[:, :, -1:] + jnp.cumsum(logf_s_t, axis=2)
    f_all_t = jnp.concatenate([f_cache_t, f_new_t], axis=2)
    bias = f_all_t[:, :, PAST_LEN:, None] - f_all_t[:, :, None, :]
    kpos = jnp.arange(PAST_LEN + DEC_SEQ)
    qpos = PAST_LEN + jnp.arange(DEC_SEQ)
    bias = jnp.where(kpos[None, :] <= qpos[:, None], bias, NEG_BIG)
    bias = bias.reshape(DEC_BATCH, FOX_HEADS * DEC_SEQ, PAST_LEN + DEC_SEQ)
    bias_n = jnp.pad(bias[:, :, PAST_LEN:], ((0, 0), (0, 0), (0, FOX_HEADS * DEC_SEQ - DEC_SEQ)),
                     constant_values=NEG_BIG)
    yb = fox_sample(qn_s, knb_s, vb_s,
                    st['fox_k'].reshape(DEPTH, DEC_BATCH, PAST_LEN * FOX_HEADS, FOX_HEAD_DIM),
                    st['fox_v'].reshape(DEPTH, DEC_BATCH, PAST_LEN * FOX_HEADS, FOX_HEAD_DIM),
                    l, bias[:, :, :PAST_LEN], bias_n, yb)

    ws = p['gmlp_ws'][l]
    bs = p['gmlp_bs'][l]
    idx = jnp.arange(GMLP_CHUNK)
    cmask = (idx[None, :] // CHUNK) <= (idx[:, None] // CHUNK)
    eye_p = jnp.eye(GMLP_STEP_CHUNKS, dtype=F32)
    w_p = jax.vmap(lambda w: jnp.kron(eye_p, w))(jnp.where(cmask[None], ws, 0.0)).astype(BF16)
    rows_p = GMLP_STEP_CHUNKS * GMLP_CHUNK
    yc = gmlp(zc, p['gmlp_v_norm'][l], w_p, jnp.tile(bs.T, (GMLP_STEP_CHUNKS, 1)), R=rows_p,
              nchunk=M_P // rows_p, row0=0, emit_vn=False)[0]
    eye = jnp.eye(DEC_BATCH, dtype=F32)
    w_s = jax.vmap(lambda w: jnp.kron(eye, w))(ws[:, :DEC_SEQ, :DEC_SEQ]).astype(BF16)
    b_s = jnp.tile(bs[:, :DEC_SEQ].T, (DEC_BATCH, 1))
    yc, vn_s = gmlp(zc, p['gmlp_v_norm'][l], w_s, b_s, R=M_S, nchunk=1, row0=M_P, emit_vn=True, ybuf=yc)

    a = -jnp.exp(p['ssm_a_log'][l])
    adt_t = dt_t * a[:, None]
    dt = dt_t.T
    qp = SSM_Q_PROMPT
    nchunk_p = SEQ // qp
    adt_t_p = adt_t[None, :, :M_P]
    adt_t_s = jnp.transpose(adt_t[:, M_P:].reshape(SSM_HEADS, DEC_BATCH, DEC_SEQ), (1, 0, 2))
    dfull = jnp.repeat(p['ssm_d'][l], SSM_HEAD_DIM).reshape(1, SSM_DIM)
    common = (p['ssm_conv_w'][l], p['ssm_conv_b'][l].reshape(1, -1), dfull, p['ssm_norm'][l].reshape(1, -1))
    npair = SSM_HEADS // 2
    yd, ncs_p, hs_p = ssm(zc, dt[:M_P], adt_t_p, lambda s, c: (0, 0, s * nchunk_p + c), a.reshape(1, -1),
                          jnp.zeros((BATCH, SSM_CONV - 1, SSM_XBC), F32),
                          jnp.zeros((BATCH, npair, 2 * SSM_HEAD_DIM, SSM_STATE), F32),
                          *common, Q=qp, nseq=BATCH, L=SEQ, row0=0)
    yd, ncs_s, hs_s = ssm(zc, dt[M_P:], adt_t_s, lambda s, c: (s, 0, 0), a.reshape(1, -1), st['conv_ssm'][l],
                          st['ssm'][l].reshape(DEC_BATCH, npair, 2 * SSM_HEAD_DIM, SSM_STATE),
                          *common, Q=DEC_SEQ, nseq=DEC_BATCH, L=DEC_SEQ, row0=M_P, ybuf=yd)

    merged = merge(h, [ya, yb, yc, yd], cast_rows(wt_in, l, row0=OFF_G, n=N_BRANCH * D_MODEL),
                   p['w_ups_bf16'], l, p['gate_bias'][l])
    x = matmul_ws(merged, p['w_out'], l, n=D_MODEL, tm=1056, tn=512, res=x, scale=1.0, name="out_proj")

    logf = logf_t.T
    hshape = (SSM_HEADS, SSM_HEAD_DIM, SSM_STATE)
    new_p = (nca_p, ncs_p, hs_p.reshape((BATCH,) + hshape), logf[:M_P].reshape(BATCH, SEQ, FOX_HEADS))
    new_s = (nca_s, ncs_s, hs_s.reshape((DEC_BATCH,) + hshape),
             kn_s.reshape(DEC_BATCH, DEC_SEQ, FOX_HEADS, FOX_HEAD_DIM),
             v_s.reshape(DEC_BATCH, DEC_SEQ, FOX_HEADS, FOX_HEAD_DIM),
             logf[M_P:].reshape(DEC_BATCH, DEC_SEQ, FOX_HEADS), vn_s.reshape(DEC_BATCH, DEC_SEQ, GMLP_DIM))
    return x, new_p, new_s, kn_stack, v_stack


def kernel(x_prompt, x_sample, cache_fox_k, cache_fox_v, cache_fox_logf, state_conv_a, state_conv_ssm, state_ssm, ffn1_norm, ffn1_w_gate, ffn1_w_up, ffn1_w_down, mix_norm, w_in, gate_bias, conv_a_w, w_up_a, fox_q_norm, fox_k_norm, fox_f_bias, w_up_b, gmlp_v_norm, gmlp_ws, gmlp_bs, w_up_c, ssm_conv_w, ssm_conv_b, ssm_dt_bias, ssm_a_log, ssm_d, ssm_norm, w_up_d, w_out, ffn2_norm, ffn2_w_gate, ffn2_w_up, ffn2_w_down):
    p = dict(mix_norm=mix_norm, w_in=w_in, gate_bias=gate_bias, conv_a_w=conv_a_w, w_up_a=w_up_a,
             fox_q_norm=fox_q_norm, fox_k_norm=fox_k_norm, fox_f_bias=fox_f_bias, w_up_b=w_up_b,
             gmlp_v_norm=gmlp_v_norm, gmlp_ws=gmlp_ws, gmlp_bs=gmlp_bs, w_up_c=w_up_c,
             ssm_conv_w=ssm_conv_w, ssm_conv_b=ssm_conv_b, ssm_dt_bias=ssm_dt_bias, ssm_a_log=ssm_a_log,
             ssm_d=ssm_d, ssm_norm=ssm_norm, w_up_d=w_up_d, w_out=w_out)
    st = dict(fox_k=cache_fox_k, fox_v=cache_fox_v, fox_logf=cache_fox_logf,
              conv_a=state_conv_a, conv_ssm=state_conv_ssm, ssm=state_ssm)
    x = jnp.concatenate([x_prompt.reshape(M_P, D_MODEL), x_sample.reshape(M_S, D_MODEL)], axis=0)
    p['w_ups_bf16'] = [w.astype(BF16) for w in (w_up_a, w_up_b, w_up_c, w_up_d)]
    p['wt_in'] = jnp.transpose(w_in, (0, 2, 1))
    wd1 = ffn1_w_down.astype(BF16)
    wd2 = ffn2_w_down.astype(BF16)
    st_p, st_s = [], []
    kn_stack = v_stack = None
    for l in range(DEPTH):
        x = ffn(x, ffn1_norm[l], ffn1_w_gate, ffn1_w_up, wd1, l)
        x, new_p, new_s, kn_stack, v_stack = _token_mix(x, l, p, st, kn_stack, v_stack)
        x = ffn(x, ffn2_norm[l], ffn2_w_gate, ffn2_w_up, wd2, l)
        st_p.append(new_p)
        st_s.append(new_s)

    def stack(states, i):
        return jnp.stack([s[i] for s in states])

    kv_shape = (DEPTH, BATCH, SEQ, FOX_HEADS, FOX_HEAD_DIM)
    return (x[:M_P].reshape(BATCH, SEQ, D_MODEL), x[M_P:].reshape(DEC_BATCH, DEC_SEQ, D_MODEL),
            kn_stack.reshape(kv_shape), v_stack.reshape(kv_shape),
            stack(st_p, 3), stack(st_p, 0), stack(st_p, 1), stack(st_p, 2),
            stack(st_s, 3), stack(st_s, 4), stack(st_s, 5), stack(st_s, 0), stack(st_s, 1), stack(st_s, 2),
            stack(st_s, 6))
```

```python
import functools
import math

import jax
import jax.numpy as jnp
from jax import lax
from jax.experimental import pallas as pl
from jax.experimental.pallas import tpu as pltpu

F32 = jnp.float32
BF16 = jnp.bfloat16

D_MODEL = 4096
BATCH = 4
SEQ = 2048
DEPTH = 4
DEC_BATCH = 16
DEC_SEQ = 16
PAST_LEN = 2048
CHUNK = 64
D_FF = 11008
EPS = 1e-6
N_BRANCH = 4
CONV_DIM = 1024
CONV_WIDTH = 3
FOX_HEADS = 8
FOX_HEAD_DIM = 128
FOX_DIM = FOX_HEADS * FOX_HEAD_DIM
GMLP_GROUPS = 8
GMLP_GROUP_DIM = 128
GMLP_DIM = GMLP_GROUPS * GMLP_GROUP_DIM
GMLP_CHUNK = 128
SSM_DIM = 1024
SSM_HEAD_DIM = 64
SSM_HEADS = SSM_DIM // SSM_HEAD_DIM
SSM_GROUPS = 2
SSM_STATE = 128
SSM_CONV = 4
SSM_XBC = SSM_DIM + 2 * SSM_GROUPS * SSM_STATE

M_P = BATCH * SEQ
M_S = DEC_BATCH * DEC_SEQ
M_ALL = M_P + M_S

OFF_F = 3 * CONV_DIM + 3 * FOX_DIM
OFF_U = OFF_F + FOX_HEADS
OFF_DT = OFF_U + 2 * GMLP_DIM + SSM_DIM + SSM_XBC
OFF_G = OFF_DT + SSM_HEADS
N_ZC = 2 * GMLP_DIM + SSM_DIM + SSM_XBC

V7X_VMEM_LIMIT_BYTES = 56 * 1024 * 1024
NEG_BIG = -1e30
ATT_SCALE = 1.0 / math.sqrt(FOX_HEAD_DIM)
SSM_Q_PROMPT = 256
GMLP_STEP_CHUNKS = 4


def _cp(*sem):
    return pltpu.CompilerParams(dimension_semantics=sem, vmem_limit_bytes=V7X_VMEM_LIMIT_BYTES)


def _silu(x):
    return x * jax.nn.sigmoid(x)


def _rmsnorm_body(x_ref, g_ref, o_ref):
    x = x_ref[...]
    ms = jnp.mean(x * x, axis=-1, keepdims=True)
    o_ref[...] = (x * lax.rsqrt(ms + EPS) * g_ref[...]).astype(o_ref.dtype)


def rmsnorm_bf16(x, g, tm=384):
    m, d = x.shape
    return pl.pallas_call(
        _rmsnorm_body,
        grid=(m // tm,),
        in_specs=[pl.BlockSpec((tm, d), lambda i: (i, 0)),
                  pl.BlockSpec((1, d), lambda i: (0, 0))],
        out_specs=pl.BlockSpec((tm, d), lambda i: (i, 0)),
        out_shape=jax.ShapeDtypeStruct((m, d), BF16),
        compiler_params=_cp("parallel"),
        name="rmsnorm",
    )(x, g.reshape(1, d))


def _mm_body(a_ref, w_ref, *refs, nk, scale, has_res):
    if has_res:
        x_ref, o_ref, *scr = refs
    else:
        x_ref = None
        o_ref, *scr = refs
    d = jnp.dot(a_ref[...], w_ref[...], preferred_element_type=F32)

    def finish(acc):
        if has_res:
            acc = x_ref[...] + scale * acc
        o_ref[...] = acc.astype(o_ref.dtype)

    if nk == 1:
        finish(d)
    else:
        acc_ref = scr[0]
        k = pl.program_id(2)

        @pl.when(k == 0)
        def _():
            acc_ref[...] = d

        @pl.when(jnp.logical_and(k > 0, k < nk - 1))
        def _():
            acc_ref[...] += d

        @pl.when(k == nk - 1)
        def _():
            finish(acc_ref[...] + d)


def matmul(a, w, *, tm, tn, nk=1, layer=None, res=None, scale=1.0, out_dtype=F32, name="matmul"):
    m, kdim = a.shape
    n = w.shape[-1]
    tk = kdim // nk
    if layer is None:
        w_spec = pl.BlockSpec((tk, tn), lambda i, j, k: (k, j))
    else:
        w_spec = pl.BlockSpec((None, tk, tn), lambda i, j, k: (layer, k, j))
    in_specs = [pl.BlockSpec((tm, tk), lambda i, j, k: (i, k)), w_spec]
    args = [a, w]
    if res is not None:
        in_specs.append(pl.BlockSpec((tm, tn), lambda i, j, k: (i, j)))
        args.append(res)
    scratch = [pltpu.VMEM((tm, tn), F32)] if nk > 1 else []
    return pl.pallas_call(
        functools.partial(_mm_body, nk=nk, scale=scale, has_res=res is not None),
        grid=(m // tm, n // tn, nk),
        in_specs=in_specs,
        out_specs=pl.BlockSpec((tm, tn), lambda i, j, k: (i, j)),
        out_shape=jax.ShapeDtypeStruct((m, n), out_dtype),
        scratch_shapes=scratch,
        compiler_params=_cp("parallel", "parallel", "arbitrary"),
        name=name,
    )(*args)


LANES = 128
SUBLANES = 8
NT_DIMS = (((1,), (1,)), ((), ()))


def _wt_rows_spec(tn, kdim, layer, row0, step_of):
    if row0 % tn == 0:
        return pl.BlockSpec((None, tn, kdim), lambda *g: (layer, row0 // tn + step_of(*g), 0))
    assert row0 % SUBLANES == 0 and tn % SUBLANES == 0

    def index_map(*g):
        row = row0 + tn * step_of(*g)
        if not isinstance(row, int):
            row = pl.multiple_of(row, SUBLANES)
        return layer, row, 0

    return pl.BlockSpec((None, pl.Element(tn), pl.Element(kdim)), index_map)


def _mm_wst_body(a_ref, w_ref, o_ref, wb_ref):
    @pl.when(pl.program_id(1) == 0)
    def _():
        wb_ref[...] = w_ref[...].astype(BF16)

    o_ref[...] = lax.dot_general(a_ref[...], wb_ref[...], NT_DIMS,
                                 preferred_element_type=F32).astype(o_ref.dtype)


def matmul_wst(a, wt_stack, layer, *, row0, n, tm, tn, out_dtype=F32, name="matmul_wst"):
    m, kdim = a.shape
    return pl.pallas_call(
        _mm_wst_body,
        grid=(n // tn, m // tm),
        in_specs=[pl.BlockSpec((tm, kdim), lambda j, i: (i, 0)),
                  _wt_rows_spec(tn, kdim, layer, row0, lambda j, i: j)],
        out_specs=pl.BlockSpec((tm, tn), lambda j, i: (i, j)),
        out_shape=jax.ShapeDtypeStruct((m, n), out_dtype),
        scratch_shapes=[pltpu.VMEM((tn, kdim), BF16)],
        compiler_params=_cp("arbitrary", "arbitrary"),
        name=name,
    )(a, wt_stack)


def _cast_rows_body(w_ref, o_ref):
    o_ref[...] = w_ref[...].astype(o_ref.dtype)


def cast_rows(wt_stack, layer, *, row0, n, tr=512):
    kdim = wt_stack.shape[2]
    return pl.pallas_call(
        _cast_rows_body,
        grid=(n // tr,),
        in_specs=[_wt_rows_spec(tr, kdim, layer, row0, lambda r: r)],
        out_specs=pl.BlockSpec((tr, kdim), lambda r: (r, 0)),
        out_shape=jax.ShapeDtypeStruct((n, kdim), BF16),
        compiler_params=_cp("parallel"),
        name="cast_rows",
    )(wt_stack)


def _in_proj_s_body(h_ref, wf_ref, wdt_ref, o_ref):
    pad = jnp.zeros((LANES - FOX_HEADS - SSM_HEADS, wf_ref.shape[1]), F32)
    w = jnp.concatenate([wf_ref[...], wdt_ref[...], pad], axis=0).astype(BF16)
    o_ref[...] = lax.dot_general(h_ref[...], w, NT_DIMS, preferred_element_type=F32)


def in_proj_small(h, wt_stack, layer, tm=1056):
    m, kdim = h.shape
    return pl.pallas_call(
        _in_proj_s_body,
        grid=(m // tm,),
        in_specs=[pl.BlockSpec((tm, kdim), lambda i: (i, 0)),
                  _wt_rows_spec(FOX_HEADS, kdim, layer, OFF_F, lambda i: 0),
                  _wt_rows_spec(SSM_HEADS, kdim, layer, OFF_DT, lambda i: 0)],
        out_specs=pl.BlockSpec((tm, LANES), lambda i: (i, 0)),
        out_shape=jax.ShapeDtypeStruct((m, LANES), F32),
        compiler_params=_cp("parallel"),
        name="in_proj_s",
    )(h, wt_stack, wt_stack)


def _mm_ws_body(a_ref, w_ref, *refs, scale, has_res):
    if has_res:
        x_ref, o_ref, wb_ref = refs
    else:
        x_ref = None
        o_ref, wb_ref = refs

    @pl.when(pl.program_id(1) == 0)
    def _():
        wb_ref[...] = w_ref[...].astype(BF16)

    d = jnp.dot(a_ref[...], wb_ref[...], preferred_element_type=F32)
    if has_res:
        d = x_ref[...] + scale * d
    o_ref[...] = d.astype(o_ref.dtype)


def matmul_ws(a, w_stack, layer, *, n, tm, tn, col0=0, res=None, scale=1.0, out_dtype=F32, name="matmul_ws"):
    m, kdim = a.shape
    cb0 = col0 // tn
    in_specs = [pl.BlockSpec((tm, kdim), lambda j, i: (i, 0)),
                pl.BlockSpec((None, kdim, tn), lambda j, i: (layer, 0, cb0 + j))]
    args = [a, w_stack]
    if res is not None:
        in_specs.append(pl.BlockSpec((tm, tn), lambda j, i: (i, j)))
        args.append(res)
    return pl.pallas_call(
        functools.partial(_mm_ws_body, scale=scale, has_res=res is not None),
        grid=(n // tn, m // tm),
        in_specs=in_specs,
        out_specs=pl.BlockSpec((tm, tn), lambda j, i: (i, j)),
        out_shape=jax.ShapeDtypeStruct((m, n), out_dtype),
        scratch_shapes=[pltpu.VMEM((kdim, tn), BF16)],
        compiler_params=_cp("arbitrary", "arbitrary"),
        name=name,
    )(*args)


def _ffn_up_body(h_ref, wg_ref, wu_ref, o_ref, wgb_ref, wub_ref):
    @pl.when(pl.program_id(1) == 0)
    def _():
        wgb_ref[...] = wg_ref[...].astype(BF16)
        wub_ref[...] = wu_ref[...].astype(BF16)

    h = h_ref[...]
    g = jnp.dot(h, wgb_ref[...], preferred_element_type=F32)
    u = jnp.dot(h, wub_ref[...], preferred_element_type=F32)
    o_ref[...] = (_silu(g) * u).astype(o_ref.dtype)


def ffn_up(h, wg_stack, wu_stack, layer, tm=1408, tn=256):
    m, d = h.shape
    n = wg_stack.shape[2]
    w_spec = pl.BlockSpec((None, d, tn), lambda j, i: (layer, 0, j))
    return pl.pallas_call(
        _ffn_up_body,
        grid=(n // tn, m // tm),
        in_specs=[pl.BlockSpec((tm, d), lambda j, i: (i, 0)), w_spec, w_spec],
        out_specs=pl.BlockSpec((tm, tn), lambda j, i: (i, j)),
        out_shape=jax.ShapeDtypeStruct((m, n), BF16),
        scratch_shapes=[pltpu.VMEM((d, tn), BF16), pltpu.VMEM((d, tn), BF16)],
        compiler_params=_cp("arbitrary", "arbitrary"),
        name="ffn_up",
    )(h, wg_stack, wu_stack)


def ffn(x, norm_g, wg_stack, wu_stack, wd_stack_bf16, layer):
    h = rmsnorm_bf16(x, norm_g)
    g = ffn_up(h, wg_stack, wu_stack, layer)
    return matmul(g, wd_stack_bf16, layer=layer, tm=1056, tn=512, nk=2, res=x, scale=0.5, name="ffn_down")


def _merge_body(h_ref, ya_ref, yb_ref, yc_ref, yd_ref, g0_ref, g1_ref, g2_ref, g3_ref,
                wa_ref, wb_ref, wc_ref, wd_ref, bias_ref, o_ref):
    h = h_ref[...]
    acc = None
    branches = ((ya_ref, g0_ref, wa_ref), (yb_ref, g1_ref, wb_ref),
                (yc_ref, g2_ref, wc_ref), (yd_ref, g3_ref, wd_ref))
    for b, (y_ref, g_ref, w_ref) in enumerate(branches):
        gate = jax.nn.sigmoid(lax.dot_general(h, g_ref[...], NT_DIMS, preferred_element_type=F32)
                              + bias_ref[b:b + 1, :])
        t = gate * jnp.dot(y_ref[...], w_ref[...], preferred_element_type=F32)
        acc = t if acc is None else acc + t
    o_ref[...] = acc.astype(o_ref.dtype)


def merge(h, ys, wt_gate, w_ups, layer, gate_bias, tm=704, tn=256):
    m, d = h.shape
    nb = D_MODEL // tn
    y_spec = pl.BlockSpec((tm, CONV_DIM), lambda i, j: (i, 0))
    in_specs = [pl.BlockSpec((tm, d), lambda i, j: (i, 0))] + [y_spec] * 4
    for b in range(N_BRANCH):
        in_specs.append(pl.BlockSpec((tn, d), lambda i, j, b=b: (b * nb + j, 0)))
    in_specs += [pl.BlockSpec((None, CONV_DIM, tn), lambda i, j: (layer, 0, j))] * 4
    in_specs.append(pl.BlockSpec((N_BRANCH, tn), lambda i, j: (0, j)))
    return pl.pallas_call(
        _merge_body,
        grid=(m // tm, nb),
        in_specs=in_specs,
        out_specs=pl.BlockSpec((tm, tn), lambda i, j: (i, j)),
        out_shape=jax.ShapeDtypeStruct((m, D_MODEL), BF16),
        compiler_params=_cp("parallel", "parallel"),
        name="merge",
    )(h, *ys, wt_gate, wt_gate, wt_gate, wt_gate, *w_ups, gate_bias)


def _alias_rows(ybuf, in_specs, args):
    if ybuf is None:
        return {}
    in_specs.append(pl.BlockSpec(memory_space=pl.ANY))
    args.append(ybuf)
    return {len(args) - 1: 0}


def _conv_a_body(xa_ref, gb_ref, gc_ref, buf_ref, w_ref, *refs, L):
    y_ref, nb_ref, scr_ref = refs[-3:]
    t = gc_ref[...] * xa_ref[...]
    scr_ref[pl.ds(8, L), :] = t
    scr_ref[pl.ds(6, 2), :] = buf_ref[0]
    w = w_ref[...]
    y = scr_ref[pl.ds(6, L), :] * w[0:1] + scr_ref[pl.ds(7, L), :] * w[1:2] + t * w[2:3]
    y_ref[...] = (gb_ref[...] * y).astype(y_ref.dtype)
    nb_ref[0] = scr_ref[pl.ds(6 + L, 2), :]


def conv_a(za, buf, w, *, L, nseq, row0, ybuf=None, cb=256):
    ncb = CONV_DIM // cb
    rb = row0 // L

    def zspec(grp):
        return pl.BlockSpec((L, cb), lambda s, c: (rb + s, grp * ncb + c))

    in_specs = [zspec(0), zspec(1), zspec(2),
                pl.BlockSpec((1, CONV_WIDTH - 1, cb), lambda s, c: (s, 0, c)),
                pl.BlockSpec((CONV_WIDTH, cb), lambda s, c: (0, c))]
    args = [za, za, za, buf, w]
    aliases = _alias_rows(ybuf, in_specs, args)
    return pl.pallas_call(
        functools.partial(_conv_a_body, L=L),
        grid=(nseq, ncb),
        in_specs=in_specs,
        out_specs=[pl.BlockSpec((L, cb), lambda s, c: (rb + s, c)),
                   pl.BlockSpec((1, CONV_WIDTH - 1, cb), lambda s, c: (s, 0, c))],
        out_shape=[jax.ShapeDtypeStruct((M_ALL, CONV_DIM), BF16),
                   jax.ShapeDtypeStruct((nseq, CONV_WIDTH - 1, CONV_DIM), F32)],
        scratch_shapes=[pltpu.VMEM((L + 8, cb), F32)],
        input_output_aliases=aliases,
        compiler_params=_cp("parallel", "parallel"),
        name="conv_a",
    )(*args)


def _qknorm_body(q_ref, k_ref, v_ref, gq_ref, gk_ref, *refs):
    qn_ref, knb_ref, vb_ref, kn_ref, vf_ref = refs[-5:]
    gq = gq_ref[...]
    gk = gk_ref[...]
    for h in range(FOX_HEADS):
        sl = slice(h * FOX_HEAD_DIM, (h + 1) * FOX_HEAD_DIM)
        q = q_ref[:, sl]
        qn = q * lax.rsqrt(jnp.mean(q * q, axis=-1, keepdims=True) + EPS) * gq
        qn_ref[:, sl] = qn.astype(qn_ref.dtype)
        k = k_ref[:, sl]
        kn = k * lax.rsqrt(jnp.mean(k * k, axis=-1, keepdims=True) + EPS) * gk
        kn_ref[:, sl] = kn
        knb_ref[:, sl] = kn.astype(knb_ref.dtype)
    v = v_ref[...]
    vf_ref[...] = v
    vb_ref[...] = v.astype(vb_ref.dtype)


def qk_norm(za, gq, gk, *, rows, row0, tm, layer=None, kn_stack=None, v_stack=None):
    rb = row0 // tm

    def zspec(grp):
        return pl.BlockSpec((tm, FOX_DIM), lambda i: (rb + i, grp))

    ospec = pl.BlockSpec((tm, FOX_DIM), lambda i: (i, 0))
    gspec = pl.BlockSpec((1, FOX_HEAD_DIM), lambda i: (0, 0))
    in_specs = [zspec(3), zspec(4), zspec(5), gspec, gspec]
    args = [za, za, za, gq.reshape(1, -1), gk.reshape(1, -1)]
    aliases = {}
    if layer is None:
        f32_spec = ospec
        f32_shape = jax.ShapeDtypeStruct((rows, FOX_DIM), F32)
    else:
        f32_spec = pl.BlockSpec((None, tm, FOX_DIM), lambda i: (layer, i, 0))
        f32_shape = jax.ShapeDtypeStruct((DEPTH, rows, FOX_DIM), F32)
        if kn_stack is not None:
            in_specs += [pl.BlockSpec(memory_space=pl.ANY)] * 2
            args += [kn_stack, v_stack]
            aliases = {5: 3, 6: 4}
    bf_shape = jax.ShapeDtypeStruct((rows, FOX_DIM), BF16)
    return pl.pallas_call(
        _qknorm_body,
        grid=(rows // tm,),
        in_specs=in_specs,
        out_specs=[ospec, ospec, ospec, f32_spec, f32_spec],
        out_shape=[bf_shape, bf_shape, bf_shape, f32_shape, f32_shape],
        input_output_aliases=aliases,
        compiler_params=_cp("parallel"),
        name="qk_norm",
    )(*args)


def _flash_body(q_ref, k_ref, v_ref, fq_ref, fkt_ref, o_ref, m_ref, l_ref, acc_ref, *, tq, tk, nk):
    qi = pl.program_id(1)
    ki = pl.program_id(2)
    lanes = FOX_HEAD_DIM

    @pl.when(ki == 0)
    def _():
        m_ref[...] = jnp.full(m_ref.shape, NEG_BIG, F32)
        l_ref[...] = jnp.zeros(l_ref.shape, F32)
        acc_ref[...] = jnp.zeros(acc_ref.shape, F32)

    def step(on_diagonal):
        fq = fq_ref[...]
        fkt = fkt_ref[...]
        if on_diagonal:
            mask = (lax.broadcasted_iota(jnp.int32, (tq, tk), 1)
                    <= lax.broadcasted_iota(jnp.int32, (tq, tk), 0))
        for h in range(FOX_HEADS):
            sl = slice(h * FOX_HEAD_DIM, (h + 1) * FOX_HEAD_DIM)
            s = lax.dot_general(q_ref[:, sl], k_ref[:, sl], (((1,), (1,)), ((), ())),
                                preferred_element_type=F32) * ATT_SCALE
            s = s + (fq[:, h:h + 1] - fkt[h:h + 1, :])
            if on_diagonal:
                s = jnp.where(mask, s, NEG_BIG)
            m_prev = m_ref[h]
            m_new = jnp.maximum(m_prev, jnp.max(s, axis=-1, keepdims=True))
            alpha = jnp.exp(m_prev - m_new)
            p = jnp.exp(s - jnp.concatenate([m_new] * (tk // lanes), axis=1))
            l_ref[h] = alpha * l_ref[h] + jnp.sum(p, axis=-1, keepdims=True)
            acc_ref[:, sl] = alpha * acc_ref[:, sl] + jnp.dot(
                p.astype(BF16), v_ref[:, sl], preferred_element_type=F32)
            m_ref[h] = m_new

    @pl.when(ki < qi)
    def _():
        step(False)

    @pl.when(ki == qi)
    def _():
        step(True)

    @pl.when(ki == nk - 1)
    def _():
        for h in range(FOX_HEADS):
            sl = slice(h * FOX_HEAD_DIM, (h + 1) * FOX_HEAD_DIM)
            o_ref[:, sl] = (acc_ref[:, sl] / l_ref[h]).astype(o_ref.dtype)


def fox_prompt(qn, knb, vb, f_cum, f_cum_t, t=512):
    nb = SEQ // t
    kv_spec = pl.BlockSpec((t, FOX_DIM), lambda b, qi, ki: (b * nb + jnp.minimum(ki, qi), 0))
    return pl.pallas_call(
        functools.partial(_flash_body, tq=t, tk=t, nk=nb),
        grid=(BATCH, nb, nb),
        in_specs=[pl.BlockSpec((t, FOX_DIM), lambda b, qi, ki: (b * nb + qi, 0)),
                  kv_spec, kv_spec,
                  pl.BlockSpec((t, FOX_HEADS), lambda b, qi, ki: (b * nb + qi, 0)),
                  pl.BlockSpec((FOX_HEADS, t), lambda b, qi, ki: (0, b * nb + jnp.minimum(ki, qi)))],
        out_specs=pl.BlockSpec((t, FOX_DIM), lambda b, qi, ki: (b * nb + qi, 0)),
        out_shape=jax.ShapeDtypeStruct((M_ALL, FOX_DIM), BF16),
        scratch_shapes=[pltpu.VMEM((FOX_HEADS, t, FOX_HEAD_DIM), F32),
                        pltpu.VMEM((FOX_HEADS, t, FOX_HEAD_DIM), F32),
                        pltpu.VMEM((t, FOX_DIM), F32)],
        compiler_params=_cp("parallel", "parallel", "arbitrary"),
        name="fox_prompt",
    )(qn, knb, vb, f_cum, f_cum_t)


def _fox_sample_body(q_ref, kc_ref, vc_ref, kn_ref, vn_ref, bc_ref, bn_ref, *refs):
    o_ref = refs[-1]
    rows = FOX_HEADS * DEC_SEQ
    q = q_ref[...]
    qrep = jnp.concatenate([q] * FOX_HEADS, axis=0)
    r_head = lax.broadcasted_iota(jnp.int32, (rows, FOX_DIM), 0) // DEC_SEQ
    c_head = lax.broadcasted_iota(jnp.int32, (rows, FOX_DIM), 1) // FOX_HEAD_DIM
    own = r_head == c_head
    qbd = jnp.where(own, qrep, jnp.zeros_like(qrep))
    nt = (((1,), (1,)), ((), ()))
    pad = jnp.zeros((rows - DEC_SEQ, FOX_DIM), BF16)
    kn = jnp.concatenate([kn_ref[...], pad], axis=0)
    vn = jnp.concatenate([vn_ref[...], pad], axis=0)

    def heads_on_lanes(c_ref):
        return jnp.concatenate(
            [c_ref[0, pl.ds(h, PAST_LEN, stride=FOX_HEADS), :].astype(BF16) for h in range(FOX_HEADS)], axis=1)

    kc = heads_on_lanes(kc_ref)
    vc = heads_on_lanes(vc_ref)
    s_c = lax.dot_general(qbd, kc, nt, preferred_element_type=F32) * ATT_SCALE + bc_ref[0]
    s_n = lax.dot_general(qbd, kn, nt, preferred_element_type=F32) * ATT_SCALE + bn_ref[0]
    m = jnp.maximum(jnp.max(s_c, axis=-1, keepdims=True), jnp.max(s_n, axis=-1, keepdims=True))
    p_c = jnp.exp(s_c - m)
    p_n = jnp.exp(s_n - m)
    inv = 1.0 / (jnp.sum(p_c, axis=-1, keepdims=True) + jnp.sum(p_n, axis=-1, keepdims=True))
    o_full = (jnp.dot((p_c * inv).astype(BF16), vc, preferred_element_type=F32)
              + jnp.dot((p_n * inv).astype(BF16), vn, preferred_element_type=F32))
    o_full = jnp.where(own, o_full, 0.0)
    o = o_full[0:DEC_SEQ]
    for h in range(1, FOX_HEADS):
        o = o + o_full[h * DEC_SEQ:(h + 1) * DEC_SEQ]
    o_ref[...] = o.astype(o_ref.dtype)


def fox_sample(qn_s, knb_s, vb_s, cache_k, cache_v, layer, bias_c, bias_n, ybuf):
    rows = FOX_HEADS * DEC_SEQ
    new_spec = pl.BlockSpec((DEC_SEQ, FOX_DIM), lambda b: (b, 0))
    cache_spec = pl.BlockSpec((None, 1, PAST_LEN * FOX_HEADS, FOX_HEAD_DIM), lambda b: (layer, b, 0, 0))
    in_specs = [new_spec, cache_spec, cache_spec, new_spec, new_spec,
                pl.BlockSpec((1, rows, PAST_LEN), lambda b: (b, 0, 0)),
                pl.BlockSpec((1, rows, rows), lambda b: (b, 0, 0))]
    args = [qn_s, cache_k, cache_v, knb_s, vb_s, bias_c, bias_n]
    aliases = _alias_rows(ybuf, in_specs, args)
    return pl.pallas_call(
        _fox_sample_body,
        grid=(DEC_BATCH,),
        in_specs=in_specs,
        out_specs=pl.BlockSpec((DEC_SEQ, FOX_DIM), lambda b: (M_P // DEC_SEQ + b, 0)),
        out_shape=jax.ShapeDtypeStruct((M_ALL, FOX_DIM), BF16),
        input_output_aliases=aliases,
        compiler_params=_cp("parallel"),
        name="fox_sample",
    )(*args)


def _gmlp_body(u_ref, vg_ref, gn_ref, w_ref, b_ref, *refs, emit_vn):
    if emit_vn:
        y_ref, vn_ref = refs[-2:]
    else:
        y_ref, vn_ref = refs[-1], None
    v = jax.nn.gelu(vg_ref[...])
    vn = v * lax.rsqrt(jnp.mean(v * v, axis=-1, keepdims=True) + EPS) * gn_ref[...]
    if emit_vn:
        vn_ref[...] = vn
    vb = vn.astype(BF16)
    bias = b_ref[...]
    for g in range(GMLP_GROUPS):
        sl = slice(g * GMLP_GROUP_DIM, (g + 1) * GMLP_GROUP_DIM)
        s = jnp.dot(w_ref[g], vb[:, sl], preferred_element_type=F32) + bias[:, g:g + 1]
        y_ref[:, sl] = (jax.nn.gelu(u_ref[:, sl]) * s).astype(y_ref.dtype)


def gmlp(zc, gn, w_eff, b_eff, *, R, nchunk, row0, emit_vn, ybuf=None):
    rb = row0 // R
    out_specs = [pl.BlockSpec((R, GMLP_DIM), lambda c: (rb + c, 0))]
    out_shape = [jax.ShapeDtypeStruct((M_ALL, GMLP_DIM), BF16)]
    if emit_vn:
        out_specs.append(pl.BlockSpec((R, GMLP_DIM), lambda c: (c, 0)))
        out_shape.append(jax.ShapeDtypeStruct((nchunk * R, GMLP_DIM), F32))
    in_specs = [pl.BlockSpec((R, GMLP_DIM), lambda c: (rb + c, 0)),
                pl.BlockSpec((R, GMLP_DIM), lambda c: (rb + c, 1)),
                pl.BlockSpec((1, GMLP_DIM), lambda c: (0, 0)),
                pl.BlockSpec((GMLP_GROUPS, R, R), lambda c: (0, 0, 0)),
                pl.BlockSpec((R, GMLP_GROUPS), lambda c: (0, 0))]
    args = [zc, zc, gn.reshape(1, -1), w_eff, b_eff]
    aliases = _alias_rows(ybuf, in_specs, args)
    return pl.pallas_call(
        functools.partial(_gmlp_body, emit_vn=emit_vn),
        grid=(nchunk,),
        in_specs=in_specs,
        out_specs=out_specs,
        out_shape=out_shape,
        input_output_aliases=aliases,
        compiler_params=_cp("parallel"),
        name="gmlp",
    )(*args)


def _ssm_body(z_ref, xbc_ref, dt_ref, adtt_ref, a_ref, cbuf_ref, h0_ref, cw_ref, cb_ref, dfull_ref, nw_ref,
              *refs, Q):
    y_ref, ncb_ref, hs_out_ref, scr_ref, hs_ref = refs[-5:]
    c = pl.program_id(1)
    nc = pl.num_programs(1)
    tail = SSM_CONV - 1
    lo = 8 - tail

    @pl.when(c == 0)
    def _():
        scr_ref[pl.ds(lo, tail), :] = cbuf_ref[0]
        hs_ref[...] = h0_ref[0]

    xbc = xbc_ref[...]
    scr_ref[pl.ds(8, Q), :] = xbc
    cw = cw_ref[...]
    conv = xbc * cw[tail:tail + 1] + cb_ref[...]
    for i in range(tail):
        conv = conv + scr_ref[pl.ds(lo + i, Q), :] * cw[i:i + 1]
    xc = _silu(conv)
    new_tail = scr_ref[pl.ds(lo + Q, tail), :]
    ncb_ref[0] = new_tail
    scr_ref[pl.ds(lo, tail), :] = new_tail

    xs = xc[:, :SSM_DIM]
    gw = SSM_STATE
    dt = dt_ref[...]
    adt = dt * a_ref[...]
    ii = lax.broadcasted_iota(jnp.int32, (Q, Q), 0)
    jj = lax.broadcasted_iota(jnp.int32, (Q, Q), 1)
    tri = ii >= jj
    acum = jnp.dot(tri.astype(F32), adt, precision=lax.Precision.HIGHEST,
                   preferred_element_type=F32)
    acum_t = jnp.dot(adtt_ref[0], (ii <= jj).astype(F32), precision=lax.Precision.HIGHEST,
                     preferred_element_type=F32)
    a_last = acum[Q - 1:Q, :]
    lane_lo = lax.broadcasted_iota(jnp.int32, (Q, 2 * SSM_HEAD_DIM), 1) < SSM_HEAD_DIM
    row_lo = lax.broadcasted_iota(jnp.int32, (2 * SSM_HEAD_DIM, SSM_STATE), 0) < SSM_HEAD_DIM
    nt = (((1,), (1,)), ((), ()))
    tn = (((0,), (0,)), ((), ()))
    dfull = dfull_ref[...]
    ys = []
    s_g = None
    for pr in range(SSM_HEADS // 2):
        g = (2 * pr) // (SSM_HEADS // SSM_GROUPS)
        b_g = xc[:, SSM_DIM + g * gw:SSM_DIM + (g + 1) * gw]
        c_g = xc[:, SSM_DIM + SSM_GROUPS * gw + g * gw:SSM_DIM + SSM_GROUPS * gw + (g + 1) * gw]
        c_gb = c_g.astype(BF16)
        if (2 * pr) % (SSM_HEADS // SSM_GROUPS) == 0:
            s_g = lax.dot_general(c_gb, b_g.astype(BF16), nt, preferred_element_type=F32)
        h0i, h1i = 2 * pr, 2 * pr + 1
        sl = slice(pr * 2 * SSM_HEAD_DIM, (pr + 1) * 2 * SSM_HEAD_DIM)
        xs_p = xs[:, sl]
        a0 = acum[:, h0i:h0i + 1]
        a1 = acum[:, h1i:h1i + 1]
        l0 = jnp.exp(jnp.where(tri, a0 - acum_t[h0i:h0i + 1, :], -jnp.inf))
        l1 = jnp.exp(jnp.where(tri, a1 - acum_t[h1i:h1i + 1, :], -jnp.inf))
        xdt = (xs_p * jnp.where(lane_lo, dt[:, h0i:h0i + 1], dt[:, h1i:h1i + 1])).astype(BF16)
        y = jnp.where(lane_lo,
                      jnp.dot((s_g * l0).astype(BF16), xdt, preferred_element_type=F32),
                      jnp.dot((s_g * l1).astype(BF16), xdt, preferred_element_type=F32))
        hs_p = hs_ref[pr]
        y_off = lax.dot_general(c_gb, hs_p.astype(BF16), nt, preferred_element_type=F32)
        y = y + y_off * jnp.exp(jnp.where(lane_lo, a0, a1)) + dfull[:, sl] * xs_p
        ys.append(y)
        al0 = a_last[:, h0i:h0i + 1]
        al1 = a_last[:, h1i:h1i + 1]
        bw0 = (b_g * jnp.exp(al0 - a0)).astype(BF16)
        bw1 = (b_g * jnp.exp(al1 - a1)).astype(BF16)
        r0 = lax.dot_general(xdt, bw0, tn, preferred_element_type=F32)
        r1 = lax.dot_general(xdt, bw1, tn, preferred_element_type=F32)
        decay = jnp.where(row_lo, jnp.exp(al0), jnp.exp(al1))
        hs_ref[pr] = hs_p * decay + jnp.where(row_lo, r0, r1)

    z = z_ref[...]
    y_all = jnp.concatenate(ys, axis=1) * _silu(z)
    gsz = SSM_DIM // SSM_GROUPS
    nw = nw_ref[...]
    for g in range(SSM_GROUPS):
        sl = slice(g * gsz, (g + 1) * gsz)
        seg = y_all[:, sl]
        y_ref[:, sl] = (seg * lax.rsqrt(jnp.mean(seg * seg, axis=-1, keepdims=True) + EPS)
                        * nw[:, sl]).astype(y_ref.dtype)

    @pl.when(c == nc - 1)
    def _():
        hs_out_ref[0] = hs_ref[...]


def ssm(zc, dt, adt_t, adt_t_index, a_row, cbuf, h0, cw, cb, dfull, nw, *, Q, nseq, L, row0, ybuf=None):
    nchunk = L // Q
    rb = row0 // Q
    npair = SSM_HEADS // 2
    in_specs = [pl.BlockSpec((Q, SSM_DIM), lambda s, c: (rb + s * nchunk + c, 2)),
                pl.BlockSpec((Q, SSM_XBC), lambda s, c: (rb + s * nchunk + c, 2)),
                pl.BlockSpec((Q, SSM_HEADS), lambda s, c: (s * nchunk + c, 0)),
                pl.BlockSpec((1, SSM_HEADS, Q), adt_t_index),
                pl.BlockSpec((1, SSM_HEADS), lambda s, c: (0, 0)),
                pl.BlockSpec((1, SSM_CONV - 1, SSM_XBC), lambda s, c: (s, 0, 0)),
                pl.BlockSpec((1, npair, 2 * SSM_HEAD_DIM, SSM_STATE), lambda s, c: (s, 0, 0, 0)),
                pl.BlockSpec((SSM_CONV, SSM_XBC), lambda s, c: (0, 0)),
                pl.BlockSpec((1, SSM_XBC), lambda s, c: (0, 0)),
                pl.BlockSpec((1, SSM_DIM), lambda s, c: (0, 0)),
                pl.BlockSpec((1, SSM_DIM), lambda s, c: (0, 0))]
    args = [zc, zc, dt, adt_t, a_row, cbuf, h0, cw, cb, dfull, nw]
    aliases = _alias_rows(ybuf, in_specs, args)
    return pl.pallas_call(
        functools.partial(_ssm_body, Q=Q),
        grid=(nseq, nchunk),
        in_specs=in_specs,
        out_specs=[pl.BlockSpec((Q, SSM_DIM), lambda s, c: (rb + s * nchunk + c, 0)),
                   pl.BlockSpec((1, SSM_CONV - 1, SSM_XBC), lambda s, c: (s, 0, 0)),
                   pl.BlockSpec((1, npair, 2 * SSM_HEAD_DIM, SSM_STATE), lambda s, c: (s, 0, 0, 0))],
        out_shape=[jax.ShapeDtypeStruct((M_ALL, SSM_DIM), BF16),
                   jax.ShapeDtypeStruct((nseq, SSM_CONV - 1, SSM_XBC), F32),
                   jax.ShapeDtypeStruct((nseq, npair, 2 * SSM_HEAD_DIM, SSM_STATE), F32)],
        scratch_shapes=[pltpu.VMEM((Q + 8, SSM_XBC), F32),
                        pltpu.VMEM((npair, 2 * SSM_HEAD_DIM, SSM_STATE), F32)],
        input_output_aliases=aliases,
        compiler_params=_cp("parallel", "arbitrary"),
        name="ssm",
    )(*args)


def _cumsum_lanes(x):
    nb = x.shape[-1] // LANES
    xb = x.reshape(x.shape[:-1] + (nb, LANES))
    tri = (jnp.arange(LANES)[:, None] <= jnp.arange(LANES)[None, :]).astype(F32)
    local = jnp.einsum('...bk,kj->...bj', xb, tri, precision=lax.Precision.HIGHEST)
    total = local[..., -1]
    carry = jnp.cumsum(total, axis=-1) - total
    return (local + carry[..., None]).reshape(x.shape)


def _token_mix(x, l, p, st, kn_stack, v_stack):
    h = rmsnorm_bf16(x, p['mix_norm'][l])
    wt_in = p['wt_in']
    za = matmul_wst(h, wt_in, l, row0=0, n=OFF_F, tm=1408, tn=512, name="in_proj_a")
    zc = matmul_wst(h, wt_in, l, row0=OFF_U, n=N_ZC, tm=1056, tn=512, name="in_proj_c")
    zs = in_proj_small(h, wt_in, l)
    zs_t = zs.T
    logf_t = jax.nn.log_sigmoid(zs_t[:FOX_HEADS] + p['fox_f_bias'][l][:, None])
    dt_t = jax.nn.softplus(zs_t[FOX_HEADS:FOX_HEADS + SSM_HEADS] + p['ssm_dt_bias'][l][:, None])

    conv_w = p['conv_a_w'][l]
    ya, nca_p = conv_a(za, jnp.zeros((BATCH, CONV_WIDTH - 1, CONV_DIM), F32), conv_w,
                       L=SEQ, nseq=BATCH, row0=0)
    ya, nca_s = conv_a(za, st['conv_a'][l], conv_w, L=DEC_SEQ, nseq=DEC_BATCH, row0=M_P, ybuf=ya, cb=CONV_DIM)

    gq, gk = p['fox_q_norm'][l], p['fox_k_norm'][l]
    qn_p, knb_p, vb_p, kn_stack, v_stack = qk_norm(za, gq, gk, rows=M_P, row0=0, tm=512, layer=l,
                                                   kn_stack=kn_stack, v_stack=v_stack)
    qn_s, knb_s, vb_s, kn_s, v_s = qk_norm(za, gq, gk, rows=M_S, row0=M_P, tm=M_S)
    f_t_p = _cumsum_lanes(logf_t[:, :M_P].reshape(FOX_HEADS, BATCH, SEQ)).reshape(FOX_HEADS, M_P)
    yb = fox_prompt(qn_p, knb_p, vb_p, f_t_p.T, f_t_p)
    logf_s_t = jnp.transpose(logf_t[:, M_P:].reshape(FOX_HEADS, DEC_BATCH, DEC_SEQ), (1, 0, 2))
    f_cache_t = _cumsum_lanes(jnp.transpose(st['fox_logf'][l], (0, 2, 1)))
    f_new_t = f_cache_t[:, :, -1:] + jnp.cumsum(logf_s_t, axis=2)
    f_all_t = jnp.concatenate([f_cache_t, f_new_t], axis=2)
    bias = f_all_t[:, :, PAST_LEN:, None] - f_all_t[:, :, None, :]
    kpos = jnp.arange(PAST_LEN + DEC_SEQ)
    qpos = PAST_LEN + jnp.arange(DEC_SEQ)
    bias = jnp.where(kpos[None, :] <= qpos[:, None], bias, NEG_BIG)
    bias = bias.reshape(DEC_BATCH, FOX_HEADS * DEC_SEQ, PAST_LEN + DEC_SEQ)
    bias_n = jnp.pad(bias[:, :, PAST_LEN:], ((0, 0), (0, 0), (0, FOX_HEADS * DEC_SEQ - DEC_SEQ)),
                     constant_values=NEG_BIG)
    yb = fox_sample(qn_s, knb_s, vb_s,
                    st['fox_k'].reshape(DEPTH, DEC_BATCH, PAST_LEN * FOX_HEADS, FOX_HEAD_DIM),
                    st['fox_v'].reshape(DEPTH, DEC_BATCH, PAST_LEN * FOX_HEADS, FOX_HEAD_DIM),
                    l, bias[:, :, :PAST_LEN], bias_n, yb)

    ws = p['gmlp_ws'][l]
    bs = p['gmlp_bs'][l]
    idx = jnp.arange(GMLP_CHUNK)
    cmask = (idx[None, :] // CHUNK) <= (idx[:, None] // CHUNK)
    eye_p = jnp.eye(GMLP_STEP_CHUNKS, dtype=F32)
    w_p = jax.vmap(lambda w: jnp.kron(eye_p, w))(jnp.where(cmask[None], ws, 0.0)).astype(BF16)
    rows_p = GMLP_STEP_CHUNKS * GMLP_CHUNK
    yc = gmlp(zc, p['gmlp_v_norm'][l], w_p, jnp.tile(bs.T, (GMLP_STEP_CHUNKS, 1)), R=rows_p,
              nchunk=M_P // rows_p, row0=0, emit_vn=False)[0]
    eye = jnp.eye(DEC_BATCH, dtype=F32)
    w_s = jax.vmap(lambda w: jnp.kron(eye, w))(ws[:, :DEC_SEQ, :DEC_SEQ]).astype(BF16)
    b_s = jnp.tile(bs[:, :DEC_SEQ].T, (DEC_BATCH, 1))
    yc, vn_s = gmlp(zc, p['gmlp_v_norm'][l], w_s, b_s, R=M_S, nchunk=1, row0=M_P, emit_vn=True, ybuf=yc)

    a = -jnp.exp(p['ssm_a_log'][l])
    adt_t = dt_t * a[:, None]
    dt = dt_t.T
    qp = SSM_Q_PROMPT
    nchunk_p = SEQ // qp
    adt_t_p = adt_t[None, :, :M_P]
    adt_t_s = jnp.transpose(adt_t[:, M_P:].reshape(SSM_HEADS, DEC_BATCH, DEC_SEQ), (1, 0, 2))
    dfull = jnp.repeat(p['ssm_d'][l], SSM_HEAD_DIM).reshape(1, SSM_DIM)
    common = (p['ssm_conv_w'][l], p['ssm_conv_b'][l].reshape(1, -1), dfull, p['ssm_norm'][l].reshape(1, -1))
    npair = SSM_HEADS // 2
    yd, ncs_p, hs_p = ssm(zc, dt[:M_P], adt_t_p, lambda s, c: (0, 0, s * nchunk_p + c), a.reshape(1, -1),
                          jnp.zeros((BATCH, SSM_CONV - 1, SSM_XBC), F32),
                          jnp.zeros((BATCH, npair, 2 * SSM_HEAD_DIM, SSM_STATE), F32),
                          *common, Q=qp, nseq=BATCH, L=SEQ, row0=0)
    yd, ncs_s, hs_s = ssm(zc, dt[M_P:], adt_t_s, lambda s, c: (s, 0, 0), a.reshape(1, -1), st['conv_ssm'][l],
                          st['ssm'][l].reshape(DEC_BATCH, npair, 2 * SSM_HEAD_DIM, SSM_STATE),
                          *common, Q=DEC_SEQ, nseq=DEC_BATCH, L=DEC_SEQ, row0=M_P, ybuf=yd)

    merged = merge(h, [ya, yb, yc, yd], cast_rows(wt_in, l, row0=OFF_G, n=N_BRANCH * D_MODEL),
                   p['w_ups_bf16'], l, p['gate_bias'][l])
    x = matmul_ws(merged, p['w_out'], l, n=D_MODEL, tm=1056, tn=512, res=x, scale=1.0, name="out_proj")

    logf = logf_t.T
    hshape = (SSM_HEADS, SSM_HEAD_DIM, SSM_STATE)
    new_p = (nca_p, ncs_p, hs_p.reshape((BATCH,) + hshape), logf[:M_P].reshape(BATCH, SEQ, FOX_HEADS))
    new_s = (nca_s, ncs_s, hs_s.reshape((DEC_BATCH,) + hshape),
             kn_s.reshape(DEC_BATCH, DEC_SEQ, FOX_HEADS, FOX_HEAD_DIM),
             v_s.reshape(DEC_BATCH, DEC_SEQ, FOX_HEADS, FOX_HEAD_DIM),
             logf[M_P:].reshape(DEC_BATCH, DEC_SEQ, FOX_HEADS), vn_s.reshape(DEC_BATCH, DEC_SEQ, GMLP_DIM))
    return x, new_p, new_s, kn_stack, v_stack


def kernel(x_prompt, x_sample, cache_fox_k, cache_fox_v, cache_fox_logf, state_conv_a, state_conv_ssm, state_ssm, ffn1_norm, ffn1_w_gate, ffn1_w_up, ffn1_w_down, mix_norm, w_in, gate_bias, conv_a_w, w_up_a, fox_q_norm, fox_k_norm, fox_f_bias, w_up_b, gmlp_v_norm, gmlp_ws, gmlp_bs, w_up_c, ssm_conv_w, ssm_conv_b, ssm_dt_bias, ssm_a_log, ssm_d, ssm_norm, w_up_d, w_out, ffn2_norm, ffn2_w_gate, ffn2_w_up, ffn2_w_down):
    p = dict(mix_norm=mix_norm, w_in=w_in, gate_bias=gate_bias, conv_a_w=conv_a_w, w_up_a=w_up_a,
             fox_q_norm=fox_q_norm, fox_k_norm=fox_k_norm, fox_f_bias=fox_f_bias, w_up_b=w_up_b,
             gmlp_v_norm=gmlp_v_norm, gmlp_ws=gmlp_ws, gmlp_bs=gmlp_bs, w_up_c=w_up_c,
             ssm_conv_w=ssm_conv_w, ssm_conv_b=ssm_conv_b, ssm_dt_bias=ssm_dt_bias, ssm_a_log=ssm_a_log,
             ssm_d=ssm_d, ssm_norm=ssm_norm, w_up_d=w_up_d, w_out=w_out)
    st = dict(fox_k=cache_fox_k, fox_v=cache_fox_v, fox_logf=cache_fox_logf,
              conv_a=state_conv_a, conv_ssm=state_conv_ssm, ssm=state_ssm)
    x = jnp.concatenate([x_prompt.reshape(M_P, D_MODEL), x_sample.reshape(M_S, D_MODEL)], axis=0)
    p['w_ups_bf16'] = [w.astype(BF16) for w in (w_up_a, w_up_b, w_up_c, w_up_d)]
    p['wt_in'] = jnp.transpose(w_in, (0, 2, 1))
    wd1 = ffn1_w_down.astype(BF16)
    wd2 = ffn2_w_down.astype(BF16)
    st_p, st_s = [], []
    kn_stack = v_stack = None
    for l in range(DEPTH):
        x = ffn(x, ffn1_norm[l], ffn1_w_gate, ffn1_w_up, wd1, l)
        x, new_p, new_s, kn_stack, v_stack = _token_mix(x, l, p, st, kn_stack, v_stack)
        x = ffn(x, ffn2_norm[l], ffn2_w_gate, ffn2_w_up, wd2, l)
        st_p.append(new_p)
        st_s.append(new_s)

    def stack(states, i):
        return jnp.stack([s[i] for s in states])

    kv_shape = (DEPTH, BATCH, SEQ, FOX_HEADS, FOX_HEAD_DIM)
    return (x[:M_P].reshape(BATCH, SEQ, D_MODEL), x[M_P:].reshape(DEC_BATCH, DEC_SEQ, D_MODEL),
            kn_stack.reshape(kv_shape), v_stack.reshape(kv_shape),
            stack(st_p, 3), stack(st_p, 0), stack(st_p, 1), stack(st_p, 2),
            stack(st_s, 3), stack(st_s, 4), stack(st_s, 5), stack(st_s, 0), stack(st_s, 1), stack(st_s, 2),
            stack(st_s, 6))
```
